```python
import math
import jax, jax.numpy as jnp
from jax import lax
import numpy as np

D_MODEL = 1024
BATCH = 1
SEQ = 16384
DEPTH = 2

D_PLE = 256
D_MIX = D_MODEL
ATTN_WIDTH = D_MIX // 2
N_DIFF_HEADS = 4
DIFF_V_DIM = ATTN_WIDTH // N_DIFF_HEADS
DIFF_QK_DIM = DIFF_V_DIM // 2
CONV_WIDTH = D_MIX - ATTN_WIDTH
N_CONV_GROUPS = 8
CONV_GROUP_DIM = CONV_WIDTH // N_CONV_GROUPS
CONV_K = 3
D_IN = 3 * ATTN_WIDTH + 3 * CONV_WIDTH
D_FF = 2816
Q_BLOCK = 128
EPS = 1e-6

kernel_name = "hymba_diffattn_shortconv_macaron_ple"


def rmsnorm(x, g):
    xf = x.astype(jnp.float32)
    y = xf * lax.rsqrt(jnp.mean(xf * xf, axis=-1, keepdims=True) + EPS)
    return (y * g.astype(jnp.float32)).astype(x.dtype)


def swiglu(x, w_gate, w_up, w_down):
    return (jax.nn.silu(x @ w_gate) * (x @ w_up)) @ w_down


def causal_diff_attention(q, k, v, lam):
    b, s, h, dv = v.shape
    n_blocks = s // Q_BLOCK
    scale = 1.0 / math.sqrt(DIFF_QK_DIM)
    k_pos = jnp.arange(s)

    def block(i):
        start = i * Q_BLOCK
        qb = lax.dynamic_slice_in_dim(q, start, Q_BLOCK, axis=1)
        sc = jnp.einsum('bqhmd,bkhmd->bhmqk', qb, k).astype(jnp.float32) * scale
        q_pos = start + jnp.arange(Q_BLOCK)
        mask = k_pos[None, :] <= q_pos[:, None]
        sc = jnp.where(mask, sc, -jnp.inf)
        a = jax.nn.softmax(sc, axis=-1)
        diff = a[:, :, 0] - lam * a[:, :, 1]
        return jnp.einsum('bhqk,bkhe->bqhe', diff.astype(v.dtype), v)

    out = lax.map(block, jnp.arange(n_blocks))
    return jnp.transpose(out, (1, 0, 2, 3, 4)).reshape(b, s, h, dv)


def causal_dwconv(u, w, bias):
    c = u.shape[-1]
    y = lax.conv_general_dilated(
        u, w[:, None, :].astype(u.dtype), window_strides=(1,),
        padding=[(CONV_K - 1, 0)], dimension_numbers=('NWC', 'WIO', 'NWC'),
        feature_group_count=c)
    return y + bias


def setup_inputs(seed: int = 0) -> dict:
    key = jax.random.key(seed)
    ks = iter(jax.random.split(key, 32))

    def nrm(shape, scale):
        return jax.random.normal(next(ks), shape, jnp.float32) * scale

    def gain(shape):
        return 1.0 + nrm(shape, 0.01)

    return {
        "x": nrm((BATCH, SEQ, D_MODEL), 1.0),
        "p": nrm((DEPTH, BATCH, SEQ, D_PLE), 1.0),
        "ffn1_norm": gain((DEPTH, D_MODEL)),
        "ffn1_w_gate": nrm((DEPTH, D_MODEL, D_FF), D_MODEL ** -0.5),
        "ffn1_w_up": nrm((DEPTH, D_MODEL, D_FF), D_MODEL ** -0.5),
        "ffn1_w_down": nrm((DEPTH, D_FF, D_MODEL), D_FF ** -0.5),
        "mix_norm": gain((DEPTH, D_MODEL)),
        "w_in": nrm((DEPTH, D_MODEL, D_IN), D_MODEL ** -0.5),
        "q_norm": gain((DEPTH, DIFF_QK_DIM)),
        "k_norm": gain((DEPTH, DIFF_QK_DIM)),
        "lambda_q1": nrm((DEPTH, DIFF_QK_DIM), 0.1),
        "lambda_k1": nrm((DEPTH, DIFF_QK_DIM), 0.1),
        "lambda_q2": nrm((DEPTH, DIFF_QK_DIM), 0.1),
        "lambda_k2": nrm((DEPTH, DIFF_QK_DIM), 0.1),
        "attn_out_norm": gain((DEPTH, DIFF_V_DIM)),
        "conv_w": nrm((DEPTH, CONV_K, CONV_WIDTH), CONV_K ** -0.5),
        "conv_b": nrm((DEPTH, CONV_WIDTH), 0.01),
        "conv_out_norm": gain((DEPTH, CONV_WIDTH)),
        "w_out": nrm((DEPTH, D_MIX, D_MODEL), D_MIX ** -0.5),
        "ffn2_norm": gain((DEPTH, D_MODEL)),
        "ffn2_w_gate": nrm((DEPTH, D_MODEL, D_FF), D_MODEL ** -0.5),
        "ffn2_w_up": nrm((DEPTH, D_MODEL, D_FF), D_MODEL ** -0.5),
        "ffn2_w_down": nrm((DEPTH, D_FF, D_MODEL), D_FF ** -0.5),
        "ple_norm": gain((DEPTH, D_MODEL)),
        "ple_w_gate": nrm((DEPTH, D_MODEL, D_MODEL), D_MODEL ** -0.5),
        "ple_w_proj": nrm((DEPTH, D_PLE, D_MODEL), D_PLE ** -0.5),
    }


def reference(x, p, ffn1_norm, ffn1_w_gate, ffn1_w_up, ffn1_w_down,
              mix_norm, w_in, q_norm, k_norm,
              lambda_q1, lambda_k1, lambda_q2, lambda_k2, attn_out_norm,
              conv_w, conv_b, conv_out_norm, w_out,
              ffn2_norm, ffn2_w_gate, ffn2_w_up, ffn2_w_down,
              ple_norm, ple_w_gate, ple_w_proj):
    b, s, _ = x.shape
    splits = [ATTN_WIDTH, 2 * ATTN_WIDTH, 3 * ATTN_WIDTH,
              3 * ATTN_WIDTH + CONV_WIDTH, 3 * ATTN_WIDTH + 2 * CONV_WIDTH]
    for i in range(DEPTH):
        lam_init = 0.8 - 0.6 * math.exp(-0.3 * i)

        x = x + 0.5 * swiglu(rmsnorm(x, ffn1_norm[i]), ffn1_w_gate[i], ffn1_w_up[i], ffn1_w_down[i])

        h = rmsnorm(x, mix_norm[i])
        z = h @ w_in[i]
        q, k, v, g_b, g_c, hc = jnp.split(z, splits, axis=-1)

        q = rmsnorm(q.reshape(b, s, N_DIFF_HEADS, 2, DIFF_QK_DIM), q_norm[i])
        k = rmsnorm(k.reshape(b, s, N_DIFF_HEADS, 2, DIFF_QK_DIM), k_norm[i])
        v = v.reshape(b, s, N_DIFF_HEADS, DIFF_V_DIM)
        lam = (jnp.exp(jnp.sum(lambda_q1[i].astype(jnp.float32) * lambda_k1[i].astype(jnp.float32)))
               - jnp.exp(jnp.sum(lambda_q2[i].astype(jnp.float32) * lambda_k2[i].astype(jnp.float32)))
               + lam_init)
        o = causal_diff_attention(q, k, v, lam)
        o = (rmsnorm(o, attn_out_norm[i]) * (1.0 - lam_init)).reshape(b, s, ATTN_WIDTH)

        yc = g_b * causal_dwconv(g_c * hc, conv_w[i], conv_b[i])
        yc = rmsnorm(yc.reshape(b, s, N_CONV_GROUPS, CONV_GROUP_DIM),
                     conv_out_norm[i].reshape(N_CONV_GROUPS, CONV_GROUP_DIM)).reshape(b, s, CONV_WIDTH)

        x = x + jnp.concatenate([o, yc], axis=-1) @ w_out[i]

        x = x + 0.5 * swiglu(rmsnorm(x, ffn2_norm[i]), ffn2_w_gate[i], ffn2_w_up[i], ffn2_w_down[i])

        gate = jax.nn.sigmoid(rmsnorm(x, ple_norm[i]) @ ple_w_gate[i])
        x = x + gate * (p[i] @ ple_w_proj[i])
    return x
```

```python
import functools
import math

import jax
import jax.numpy as jnp
from jax import lax
from jax.experimental import pallas as pl
from jax.experimental.pallas import tpu as pltpu

D_MODEL = 1024
D_FF = 2816
D_PLE = 256
ATTN_WIDTH = 512
CONV_WIDTH = 512
N_HEADS = 4
HEAD_DIM = 128
QK_DIM = 64
GROUP_DIM = 64
CONV_K = 3
EPS = 1e-6

ROW_TILE = 256
ATTN_TILE = 512
CARRY_ROWS = 8
VMEM_LIMIT_BYTES = 56 * 1024 * 1024

BF16 = jnp.bfloat16
F32 = jnp.float32


def _dot(a, b):
    return jnp.dot(a, b, preferred_element_type=F32)


def _rms(x, g):
    return x * lax.rsqrt(jnp.mean(x * x, axis=-1, keepdims=True) + EPS) * g


def _group_rms(x, g, ind):
    sq = x * x
    hi = sq.astype(BF16)
    lo = (sq - hi.astype(F32)).astype(BF16)
    ms = (_dot(hi, ind) + _dot(lo, ind)) * (1.0 / GROUP_DIM)
    return x * lax.rsqrt(ms + EPS) * g


def _swiglu_half_step(x, g_ref, wg_ref, wu_ref, wd_ref):
    n = _rms(x, g_ref[...]).astype(BF16)
    gate = _dot(n, wg_ref[...])
    up = _dot(n, wu_ref[...])
    h = (gate * jax.nn.sigmoid(gate) * up).astype(BF16)
    return x + 0.5 * _dot(h, wd_ref[...])


def _pre_kernel(x_ref, g1_ref, wg_ref, wu_ref, wd_ref, gm_ref, win_ref, qn_ref, kn_ref,
                cw_ref, cb_ref, cn_ref, ind_ref,
                x1_ref, q_ref, k_ref, v_ref, yc_ref, u_ext):
    tm = x_ref.shape[0]
    x1 = _swiglu_half_step(x_ref[...], g1_ref, wg_ref, wu_ref, wd_ref)
    x1_ref[...] = x1
    h = _rms(x1, gm_ref[...]).astype(BF16)
    ind = ind_ref[...]

    def z_cols(section):
        return _dot(h, win_ref[:, section * ATTN_WIDTH:(section + 1) * ATTN_WIDTH])

    q = _group_rms(z_cols(0), qn_ref[...], ind) * (1.0 / math.sqrt(QK_DIM))
    k = _group_rms(z_cols(1), kn_ref[...], ind)
    v = z_cols(2)
    for hd in range(N_HEADS):
        cols = slice(hd * HEAD_DIM, (hd + 1) * HEAD_DIM)
        q_ref[hd] = q[:, cols].astype(BF16)
        k_ref[hd] = k[:, cols].astype(BF16)
        v_ref[hd] = v[:, cols].astype(BF16)

    @pl.when(pl.program_id(0) == 0)
    def _():
        u_ext[0:CARRY_ROWS, :] = jnp.zeros((CARRY_ROWS, CONV_WIDTH), F32)

    u = z_cols(4) * z_cols(5)
    u_ext[CARRY_ROWS:CARRY_ROWS + tm, :] = u
    conv = (cw_ref[0:1, :] * u_ext[pl.ds(CARRY_ROWS - 2, tm), :]
            + cw_ref[1:2, :] * u_ext[pl.ds(CARRY_ROWS - 1, tm), :]
            + cw_ref[2:3, :] * u + cb_ref[...])
    u_ext[0:CARRY_ROWS, :] = u_ext[tm:tm + CARRY_ROWS, :]
    yc = z_cols(3) * conv
    yc_ref[...] = _group_rms(yc, cn_ref[...], ind).astype(BF16)


def _post_kernel(x1_ref, o_ref, yc_ref, p_ref, woa_ref, wob_ref, g2_ref, wg_ref, wu_ref, wd_ref,
                 gp_ref, wpg_ref, wpp_ref, out_ref):
    x2 = x1_ref[...] + _dot(o_ref[...], woa_ref[...]) + _dot(yc_ref[...], wob_ref[...])
    x3 = _swiglu_half_step(x2, g2_ref, wg_ref, wu_ref, wd_ref)
    gate = jax.nn.sigmoid(_dot(_rms(x3, gp_ref[...]).astype(BF16), wpg_ref[...]))
    out_ref[...] = x3 + gate * _dot(p_ref[...].astype(BF16), wpp_ref[...])


def _attn_kernel(q_ref, k_ref, v_ref, lq1_ref, lk1_ref, lq2_ref, lk2_ref, on_ref, o_ref,
                 qs, m_sc, l_sc, acc, *, lam_init):
    t = ATTN_TILE
    i = pl.program_id(1)
    q = q_ref[...]
    lane = lax.broadcasted_iota(jnp.int32, q.shape, 1)
    zero = jnp.zeros_like(q)
    qs[0:t, :] = jnp.where(lane < QK_DIM, q, zero)
    qs[t:2 * t, :] = jnp.where(lane >= QK_DIM, q, zero)
    m_sc[...] = jnp.full(m_sc.shape, -jnp.inf, F32)
    l_sc[...] = jnp.zeros(l_sc.shape, F32)
    acc[...] = jnp.zeros(acc.shape, F32)

    def step(j, masked):
        start = pl.multiple_of(j * t, t)
        kj = k_ref[pl.ds(start, t), :]
        vj = v_ref[pl.ds(start, t), :]
        s = lax.dot_general(qs[...], kj, (((1,), (1,)), ((), ())), preferred_element_type=F32)
        if masked:
            row = lax.broadcasted_iota(jnp.int32, s.shape, 0)
            col = lax.broadcasted_iota(jnp.int32, s.shape, 1)
            row = jnp.where(row >= t, row - t, row)
            s = jnp.where(col <= row, s, -jnp.inf)
        m_prev = m_sc[...]
        m_new = jnp.maximum(m_prev, jnp.max(s, axis=-1, keepdims=True))
        alpha = jnp.exp(m_prev - m_new)
        p = jnp.exp(s - m_new)
        l_sc[...] = alpha * l_sc[...] + jnp.sum(p, axis=-1, keepdims=True)
        acc[...] = alpha * acc[...] + _dot(p.astype(BF16), vj)
        m_sc[...] = m_new

    def body(j, carry):
        step(j, masked=False)
        return carry

    lax.fori_loop(0, i, body, 0)
    step(i, masked=True)

    lam = (jnp.exp(jnp.sum(lq1_ref[...] * lk1_ref[...], axis=-1, keepdims=True))
           - jnp.exp(jnp.sum(lq2_ref[...] * lk2_ref[...], axis=-1, keepdims=True)) + lam_init)
    o = acc[0:t, :] / l_sc[0:t, :] - lam * (acc[t:2 * t, :] / l_sc[t:2 * t, :])
    o_ref[...] = (_rms(o, on_ref[...]) * (1.0 - lam_init)).astype(BF16)


def _resident(shape):
    nd = len(shape)
    return pl.BlockSpec(shape, lambda *_: (0,) * nd, pipeline_mode=pl.Buffered(1))


def _rows(width, tm=ROW_TILE):
    return pl.BlockSpec((tm, width), lambda i: (i, 0))


def _pre_call(x, g1, wg, wu, wd, gm, win, qn, kn, cw, cb, cn, ind):
    s = x.shape[0]
    heads = pl.BlockSpec((N_HEADS, ROW_TILE, HEAD_DIM), lambda i: (0, i, 0))
    consts = (g1, wg, wu, wd, gm, win, qn, kn, cw, cb, cn, ind)
    return pl.pallas_call(
        _pre_kernel,
        grid=(s // ROW_TILE,),
        in_specs=[_rows(D_MODEL)] + [_resident(c.shape) for c in consts],
        out_specs=[_rows(D_MODEL), heads, heads, heads, _rows(CONV_WIDTH)],
        out_shape=[jax.ShapeDtypeStruct((s, D_MODEL), F32)]
        + [jax.ShapeDtypeStruct((N_HEADS, s, HEAD_DIM), BF16)] * 3
        + [jax.ShapeDtypeStruct((s, CONV_WIDTH), BF16)],
        scratch_shapes=[pltpu.VMEM((ROW_TILE + CARRY_ROWS, CONV_WIDTH), F32)],
        compiler_params=pltpu.CompilerParams(
            dimension_semantics=("arbitrary",), vmem_limit_bytes=VMEM_LIMIT_BYTES),
        name="pre_mix",
    )(x, *consts)


def _post_call(x1, o, yc, p, woa, wob, g2, wg, wu, wd, gp, wpg, wpp):
    s = x1.shape[0]
    consts = (woa, wob, g2, wg, wu, wd, gp, wpg, wpp)
    return pl.pallas_call(
        _post_kernel,
        grid=(s // ROW_TILE,),
        in_specs=[_rows(D_MODEL), _rows(ATTN_WIDTH), _rows(CONV_WIDTH), _rows(D_PLE)]
        + [_resident(c.shape) for c in consts],
        out_specs=_rows(D_MODEL),
        out_shape=jax.ShapeDtypeStruct((s, D_MODEL), F32),
        compiler_params=pltpu.CompilerParams(
            dimension_semantics=("parallel",), vmem_limit_bytes=VMEM_LIMIT_BYTES),
        name="post_mix",
    )(x1, o, yc, p, *consts)


def _attn_call(q, k, v, lq1, lk1, lq2, lk2, on, lam_init):
    s = q.shape[1]
    t = ATTN_TILE
    q_spec = pl.BlockSpec((None, t, HEAD_DIM), lambda h, i: (h, i, 0))
    kv_spec = pl.BlockSpec((None, s, HEAD_DIM), lambda h, i: (h, 0, 0))
    vec = pl.BlockSpec((1, QK_DIM), lambda h, i: (0, 0))
    return pl.pallas_call(
        functools.partial(_attn_kernel, lam_init=lam_init),
        grid=(N_HEADS, s // t),
        in_specs=[q_spec, kv_spec, kv_spec, vec, vec, vec, vec,
                  pl.BlockSpec((1, HEAD_DIM), lambda h, i: (0, 0))],
        out_specs=pl.BlockSpec((t, HEAD_DIM), lambda h, i: (i, h)),
        out_shape=jax.ShapeDtypeStruct((s, ATTN_WIDTH), BF16),
        scratch_shapes=[pltpu.VMEM((2 * t, HEAD_DIM), BF16),
                        pltpu.VMEM((2 * t, 1), F32),
                        pltpu.VMEM((2 * t, 1), F32),
                        pltpu.VMEM((2 * t, HEAD_DIM), F32)],
        compiler_params=pltpu.CompilerParams(
            dimension_semantics=("parallel", "parallel"), vmem_limit_bytes=VMEM_LIMIT_BYTES),
        name="diff_attn",
    )(q, k, v, lq1, lk1, lq2, lk2, on)


def kernel(x, p, ffn1_norm, ffn1_w_gate, ffn1_w_up, ffn1_w_down, mix_norm, w_in, q_norm, k_norm,
           lambda_q1, lambda_k1, lambda_q2, lambda_k2, attn_out_norm, conv_w, conv_b, conv_out_norm,
           w_out, ffn2_norm, ffn2_w_gate, ffn2_w_up, ffn2_w_down, ple_norm, ple_w_gate, ple_w_proj):
    b, s, _ = x.shape
    depth = p.shape[0]
    assert b == 1 and s % ATTN_TILE == 0 and s % ROW_TILE == 0
    group = jnp.arange(ATTN_WIDTH, dtype=jnp.int32) // GROUP_DIM
    ind = (group[:, None] == group[None, :]).astype(BF16)

    def row(a):
        return a.reshape(1, -1).astype(F32)

    def tiled(a):
        return jnp.tile(a.astype(F32), ATTN_WIDTH // a.shape[0]).reshape(1, ATTN_WIDTH)

    xs = x.reshape(s, D_MODEL)
    for i in range(depth):
        lam_init = 0.8 - 0.6 * math.exp(-0.3 * i)
        x1, q, k, v, yc = _pre_call(
            xs, row(ffn1_norm[i]), ffn1_w_gate[i].astype(BF16), ffn1_w_up[i].astype(BF16),
            ffn1_w_down[i].astype(BF16), row(mix_norm[i]), w_in[i].astype(BF16),
            tiled(q_norm[i]), tiled(k_norm[i]), conv_w[i].astype(F32), row(conv_b[i]),
            row(conv_out_norm[i]), ind)
        o = _attn_call(q, k, v, row(lambda_q1[i]), row(lambda_k1[i]), row(lambda_q2[i]),
                       row(lambda_k2[i]), row(attn_out_norm[i]), lam_init)
        w_o = w_out[i].astype(BF16)
        xs = _post_call(
            x1, o, yc, p[i].reshape(s, D_PLE), w_o[:ATTN_WIDTH], w_o[ATTN_WIDTH:],
            row(ffn2_norm[i]), ffn2_w_gate[i].astype(BF16), ffn2_w_up[i].astype(BF16),
            ffn2_w_down[i].astype(BF16), row(ple_norm[i]), ple_w_gate[i].astype(BF16),
            ple_w_proj[i].astype(BF16))
    return xs.reshape(b, s, D_MODEL)
```

```python
import functools
import math

import jax
import jax.numpy as jnp
from jax import lax
from jax.experimental import pallas as pl
from jax.experimental.pallas import tpu as pltpu

D_MODEL = 1024
D_FF = 2816
D_PLE = 256
ATTN_WIDTH = 512
CONV_WIDTH = 512
N_HEADS = 4
HEAD_DIM = 128
QK_DIM = 64
GROUP_DIM = 64
CONV_K = 3
EPS = 1e-6

ROW_TILE = 256
Q_TILE = 512
KV_TILE = 512
CARRY_ROWS = 8
VMEM_LIMIT_BYTES = 56 * 1024 * 1024

BF16 = jnp.bfloat16
F32 = jnp.float32


def _dot(a, b):
    return jnp.dot(a, b, preferred_element_type=F32)


def _rms(x, g):
    return x * lax.rsqrt(jnp.mean(x * x, axis=-1, keepdims=True) + EPS) * g


def _group_rms(x, g, ind):
    sq = x * x
    hi = sq.astype(BF16)
    lo = (sq - hi.astype(F32)).astype(BF16)
    ms = (_dot(hi, ind) + _dot(lo, ind)) * (1.0 / GROUP_DIM)
    return x * lax.rsqrt(ms + EPS) * g


def _swiglu_half_step(x, g_ref, wg_ref, wu_ref, wd_ref):
    n = _rms(x, g_ref[...]).astype(BF16)
    gate = _dot(n, wg_ref[...])
    up = _dot(n, wu_ref[...])
    h = (gate * jax.nn.sigmoid(gate) * up).astype(BF16)
    return x + 0.5 * _dot(h, wd_ref[...])


def _pre_kernel(x_ref, g1_ref, wg_ref, wu_ref, wd_ref, gm_ref, win_ref, qn_ref, kn_ref,
                cw_ref, cb_ref, cn_ref, ind_ref,
                x1_ref, q_ref, k_ref, v_ref, yc_ref, u_ext):
    tm = x_ref.shape[0]
    x1 = _swiglu_half_step(x_ref[...], g1_ref, wg_ref, wu_ref, wd_ref)
    x1_ref[...] = x1
    h = _rms(x1, gm_ref[...]).astype(BF16)
    ind = ind_ref[...]

    def z_cols(section):
        return _dot(h, win_ref[:, section * ATTN_WIDTH:(section + 1) * ATTN_WIDTH])

    q = _group_rms(z_cols(0), qn_ref[...], ind) * (1.0 / math.sqrt(QK_DIM))
    k = _group_rms(z_cols(1), kn_ref[...], ind)
    v = z_cols(2)
    for hd in range(N_HEADS):
        cols = slice(hd * HEAD_DIM, (hd + 1) * HEAD_DIM)
        q_ref[hd] = q[:, cols].T.astype(BF16)
        k_ref[hd] = k[:, cols].astype(BF16)
        v_ref[hd] = v[:, cols].T.astype(BF16)

    @pl.when(pl.program_id(0) == 0)
    def _():
        u_ext[0:CARRY_ROWS, :] = jnp.zeros((CARRY_ROWS, CONV_WIDTH), F32)

    u = z_cols(4) * z_cols(5)
    u_ext[CARRY_ROWS:CARRY_ROWS + tm, :] = u
    conv = (cw_ref[0:1, :] * u_ext[pl.ds(CARRY_ROWS - 2, tm), :]
            + cw_ref[1:2, :] * u_ext[pl.ds(CARRY_ROWS - 1, tm), :]
            + cw_ref[2:3, :] * u + cb_ref[...])
    u_ext[0:CARRY_ROWS, :] = u_ext[tm:tm + CARRY_ROWS, :]
    yc = z_cols(3) * conv
    yc_ref[...] = _group_rms(yc, cn_ref[...], ind).astype(BF16)


def _post_kernel(x1_ref, o_ref, yc_ref, p_ref, woa_ref, wob_ref, g2_ref, wg_ref, wu_ref, wd_ref,
                 gp_ref, wpg_ref, wpp_ref, out_ref):
    x2 = x1_ref[...] + _dot(o_ref[...], woa_ref[...]) + _dot(yc_ref[...], wob_ref[...])
    x3 = _swiglu_half_step(x2, g2_ref, wg_ref, wu_ref, wd_ref)
    gate = jax.nn.sigmoid(_dot(_rms(x3, gp_ref[...]).astype(BF16), wpg_ref[...]))
    out_ref[...] = x3 + gate * _dot(p_ref[...].astype(BF16), wpp_ref[...])


def _attn_kernel(qt_ref, k_ref, vt_ref, lq1_ref, lk1_ref, lq2_ref, lk2_ref, on_ref, o_ref,
                 qm, m_sc, l_sc, acc, *, lam_init):
    tq, tk = Q_TILE, KV_TILE
    i = pl.program_id(1)
    qt = qt_ref[...]
    feat = lax.broadcasted_iota(jnp.int32, qt.shape, 0)
    zero = jnp.zeros_like(qt)
    qm[:, 0:tq] = jnp.where(feat < QK_DIM, qt, zero)
    qm[:, tq:2 * tq] = jnp.where(feat >= QK_DIM, qt, zero)
    m_sc[...] = jnp.full(m_sc.shape, -jnp.inf, F32)
    l_sc[...] = jnp.zeros(l_sc.shape, F32)
    acc[...] = jnp.zeros(acc.shape, F32)

    def step(j, masked):
        start = pl.multiple_of(j * tk, tk)
        kj = k_ref[pl.ds(start, tk), :]
        vtj = vt_ref[:, pl.ds(start, tk)]
        st = _dot(kj, qm[...])
        if masked:
            kpos = lax.broadcasted_iota(jnp.int32, st.shape, 0)
            qpos = lax.broadcasted_iota(jnp.int32, st.shape, 1)
            qpos = jnp.where(qpos >= tq, qpos - tq, qpos)
            st = jnp.where(kpos <= qpos, st, -jnp.inf)
        m_prev = m_sc[...]
        m_new = jnp.maximum(m_prev, jnp.max(st, axis=0, keepdims=True))
        alpha = jnp.exp(m_prev - m_new)
        p = jnp.exp(st - m_new)
        l_sc[...] = alpha * l_sc[...] + jnp.sum(p, axis=0, keepdims=True)
        acc[...] = alpha * acc[...] + _dot(vtj, p.astype(BF16))
        m_sc[...] = m_new

    def body(j, carry):
        step(j, masked=False)
        return carry

    lax.fori_loop(0, i, body, 0)
    step(i, masked=True)

    lam = (jnp.exp(jnp.sum(lq1_ref[...] * lk1_ref[...], axis=-1, keepdims=True))
           - jnp.exp(jnp.sum(lq2_ref[...] * lk2_ref[...], axis=-1, keepdims=True)) + lam_init)
    ot = acc[:, 0:tq] / l_sc[:, 0:tq] - lam * (acc[:, tq:2 * tq] / l_sc[:, tq:2 * tq])
    ot = ot * lax.rsqrt(jnp.mean(ot * ot, axis=0, keepdims=True) + EPS) * on_ref[...]
    o_ref[...] = (ot * (1.0 - lam_init)).T.astype(BF16)


def _resident(shape):
    nd = len(shape)
    return pl.BlockSpec(shape, lambda *_: (0,) * nd, pipeline_mode=pl.Buffered(1))


def _rows(width, tm=ROW_TILE):
    return pl.BlockSpec((tm, width), lambda i: (i, 0))


def _pre_call(x, g1, wg, wu, wd, gm, win, qn, kn, cw, cb, cn, ind):
    s = x.shape[0]
    row_major = pl.BlockSpec((N_HEADS, ROW_TILE, HEAD_DIM), lambda i: (0, i, 0))
    feat_major = pl.BlockSpec((N_HEADS, HEAD_DIM, ROW_TILE), lambda i: (0, 0, i))
    consts = (g1, wg, wu, wd, gm, win, qn, kn, cw, cb, cn, ind)
    return pl.pallas_call(
        _pre_kernel,
        grid=(s // ROW_TILE,),
        in_specs=[_rows(D_MODEL)] + [_resident(c.shape) for c in consts],
        out_specs=[_rows(D_MODEL), feat_major, row_major, feat_major, _rows(CONV_WIDTH)],
        out_shape=[jax.ShapeDtypeStruct((s, D_MODEL), F32),
                   jax.ShapeDtypeStruct((N_HEADS, HEAD_DIM, s), BF16),
                   jax.ShapeDtypeStruct((N_HEADS, s, HEAD_DIM), BF16),
                   jax.ShapeDtypeStruct((N_HEADS, HEAD_DIM, s), BF16),
                   jax.ShapeDtypeStruct((s, CONV_WIDTH), BF16)],
        scratch_shapes=[pltpu.VMEM((ROW_TILE + CARRY_ROWS, CONV_WIDTH), F32)],
        compiler_params=pltpu.CompilerParams(
            dimension_semantics=("arbitrary",), vmem_limit_bytes=VMEM_LIMIT_BYTES),
        name="pre_mix",
    )(x, *consts)


def _post_call(x1, o, yc, p, woa, wob, g2, wg, wu, wd, gp, wpg, wpp):
    s = x1.shape[0]
    consts = (woa, wob, g2, wg, wu, wd, gp, wpg, wpp)
    return pl.pallas_call(
        _post_kernel,
        grid=(s // ROW_TILE,),
        in_specs=[_rows(D_MODEL), _rows(ATTN_WIDTH), _rows(CONV_WIDTH), _rows(D_PLE)]
        + [_resident(c.shape) for c in consts],
        out_specs=_rows(D_MODEL),
        out_shape=jax.ShapeDtypeStruct((s, D_MODEL), F32),
        compiler_params=pltpu.CompilerParams(
            dimension_semantics=("parallel",), vmem_limit_bytes=VMEM_LIMIT_BYTES),
        name="post_mix",
    )(x1, o, yc, p, *consts)


def _attn_call(qt, k, vt, lq1, lk1, lq2, lk2, on, lam_init):
    s = k.shape[1]
    qt_spec = pl.BlockSpec((None, HEAD_DIM, Q_TILE), lambda h, i: (h, 0, i))
    k_spec = pl.BlockSpec((None, s, HEAD_DIM), lambda h, i: (h, 0, 0))
    vt_spec = pl.BlockSpec((None, HEAD_DIM, s), lambda h, i: (h, 0, 0))
    vec = pl.BlockSpec((1, QK_DIM), lambda h, i: (0, 0))
    return pl.pallas_call(
        functools.partial(_attn_kernel, lam_init=lam_init),
        grid=(N_HEADS, s // Q_TILE),
        in_specs=[qt_spec, k_spec, vt_spec, vec, vec, vec, vec,
                  pl.BlockSpec((HEAD_DIM, 1), lambda h, i: (0, 0))],
        out_specs=pl.BlockSpec((Q_TILE, HEAD_DIM), lambda h, i: (i, h)),
        out_shape=jax.ShapeDtypeStruct((s, ATTN_WIDTH), BF16),
        scratch_shapes=[pltpu.VMEM((HEAD_DIM, 2 * Q_TILE), BF16),
                        pltpu.VMEM((1, 2 * Q_TILE), F32),
                        pltpu.VMEM((1, 2 * Q_TILE), F32),
                        pltpu.VMEM((HEAD_DIM, 2 * Q_TILE), F32)],
        compiler_params=pltpu.CompilerParams(
            dimension_semantics=("parallel", "parallel"), vmem_limit_bytes=VMEM_LIMIT_BYTES),
        name="diff_attn",
    )(qt, k, vt, lq1, lk1, lq2, lk2, on)


def kernel(x, p, ffn1_norm, ffn1_w_gate, ffn1_w_up, ffn1_w_down, mix_norm, w_in, q_norm, k_norm,
           lambda_q1, lambda_k1, lambda_q2, lambda_k2, attn_out_norm, conv_w, conv_b, conv_out_norm,
           w_out, ffn2_norm, ffn2_w_gate, ffn2_w_up, ffn2_w_down, ple_norm, ple_w_gate, ple_w_proj):
    b, s, _ = x.shape
    depth = p.shape[0]
    assert b == 1 and s % Q_TILE == 0 and s % ROW_TILE == 0 and Q_TILE == KV_TILE
    group = jnp.arange(ATTN_WIDTH, dtype=jnp.int32) // GROUP_DIM
    ind = (group[:, None] == group[None, :]).astype(BF16)

    def row(a):
        return a.reshape(1, -1).astype(F32)

    def tiled(a):
        return jnp.tile(a.astype(F32), ATTN_WIDTH // a.shape[0]).reshape(1, ATTN_WIDTH)

    xs = x.reshape(s, D_MODEL)
    for i in range(depth):
        lam_init = 0.8 - 0.6 * math.exp(-0.3 * i)
        x1, qt, k, vt, yc = _pre_call(
            xs, row(ffn1_norm[i]), ffn1_w_gate[i].astype(BF16), ffn1_w_up[i].astype(BF16),
            ffn1_w_down[i].astype(BF16), row(mix_norm[i]), w_in[i].astype(BF16),
            tiled(q_norm[i]), tiled(k_norm[i]), conv_w[i].astype(F32), row(conv_b[i]),
            row(conv_out_norm[i]), ind)
        o = _attn_call(qt, k, vt, row(lambda_q1[i]), row(lambda_k1[i]), row(lambda_q2[i]),
                       row(lambda_k2[i]), attn_out_norm[i].astype(F32).reshape(HEAD_DIM, 1), lam_init)
        w_o = w_out[i].astype(BF16)
        xs = _post_call(
            x1, o, yc, p[i].reshape(s, D_PLE), w_o[:ATTN_WIDTH], w_o[ATTN_WIDTH:],
            row(ffn2_norm[i]), ffn2_w_gate[i].astype(BF16), ffn2_w_up[i].astype(BF16),
            ffn2_w_down[i].astype(BF16), row(ple_norm[i]), ple_w_gate[i].astype(BF16),
            ple_w_proj[i].astype(BF16))
    return xs.reshape(b, s, D_MODEL)
```

```python
import functools
import math

import jax
import jax.numpy as jnp
from jax import lax
from jax.experimental import pallas as pl
from jax.experimental.pallas import tpu as pltpu

D_MODEL = 1024
D_FF = 2816
D_PLE = 256
ATTN_WIDTH = 512
CONV_WIDTH = 512
N_HEADS = 4
HEAD_DIM = 128
QK_DIM = 64
GROUP_DIM = 64
CONV_K = 3
EPS = 1e-6

ROW_TILE = 256
Q_TILE = 512
KV_TILE = 512
CARRY_ROWS = 8
VMEM_LIMIT_BYTES = 56 * 1024 * 1024

BF16 = jnp.bfloat16
F32 = jnp.float32


def _dot(a, b):
    return jnp.dot(a, b, preferred_element_type=F32)


def _rms(x, g):
    return x * lax.rsqrt(jnp.mean(x * x, axis=-1, keepdims=True) + EPS) * g


def _group_rms(x, g, ind):
    sq = x * x
    hi = sq.astype(BF16)
    lo = (sq - hi.astype(F32)).astype(BF16)
    ms = (_dot(hi, ind) + _dot(lo, ind)) * (1.0 / GROUP_DIM)
    return x * lax.rsqrt(ms + EPS) * g


def _swiglu_half_step(x, g_ref, wg_ref, wu_ref, wd_ref):
    n = _rms(x, g_ref[...]).astype(BF16)
    gate = _dot(n, wg_ref[...])
    up = _dot(n, wu_ref[...])
    h = (gate * jax.nn.sigmoid(gate) * up).astype(BF16)
    return x + 0.5 * _dot(h, wd_ref[...])


def _pre_kernel(x_ref, g1_ref, wg_ref, wu_ref, wd_ref, gm_ref, win_ref, qn_ref, kn_ref,
                cw_ref, cb_ref, cn_ref, ind_ref,
                x1_ref, q_ref, k_ref, v_ref, yc_ref, u_ext):
    tm = x_ref.shape[0]
    x1 = _swiglu_half_step(x_ref[...], g1_ref, wg_ref, wu_ref, wd_ref)
    x1_ref[...] = x1
    h = _rms(x1, gm_ref[...]).astype(BF16)
    ind = ind_ref[...]

    def z_cols(section):
        return _dot(h, win_ref[:, section * ATTN_WIDTH:(section + 1) * ATTN_WIDTH])

    q = _group_rms(z_cols(0), qn_ref[...], ind) * (math.log2(math.e) / math.sqrt(QK_DIM))
    k = _group_rms(z_cols(1), kn_ref[...], ind)
    v = z_cols(2)
    for hd in range(N_HEADS):
        cols = slice(hd * HEAD_DIM, (hd + 1) * HEAD_DIM)
        q_ref[hd] = q[:, cols].T.astype(BF16)
        k_ref[hd] = k[:, cols].astype(BF16)
        v_ref[hd] = v[:, cols].T.astype(BF16)

    @pl.when(pl.program_id(0) == 0)
    def _():
        u_ext[0:CARRY_ROWS, :] = jnp.zeros((CARRY_ROWS, CONV_WIDTH), F32)

    u = z_cols(4) * z_cols(5)
    u_ext[CARRY_ROWS:CARRY_ROWS + tm, :] = u
    conv = (cw_ref[0:1, :] * u_ext[pl.ds(CARRY_ROWS - 2, tm), :]
            + cw_ref[1:2, :] * u_ext[pl.ds(CARRY_ROWS - 1, tm), :]
            + cw_ref[2:3, :] * u + cb_ref[...])
    u_ext[0:CARRY_ROWS, :] = u_ext[tm:tm + CARRY_ROWS, :]
    yc = z_cols(3) * conv
    yc_ref[...] = _group_rms(yc, cn_ref[...], ind).astype(BF16)


def _post_kernel(x1_ref, o_ref, yc_ref, p_ref, woa_ref, wob_ref, g2_ref, wg_ref, wu_ref, wd_ref,
                 gp_ref, wpg_ref, wpp_ref, out_ref):
    x2 = x1_ref[...] + _dot(o_ref[...], woa_ref[...]) + _dot(yc_ref[...], wob_ref[...])
    x3 = _swiglu_half_step(x2, g2_ref, wg_ref, wu_ref, wd_ref)
    gate = jax.nn.sigmoid(_dot(_rms(x3, gp_ref[...]).astype(BF16), wpg_ref[...]))
    out_ref[...] = x3 + gate * _dot(p_ref[...].astype(BF16), wpp_ref[...])


def _attn_kernel(qt_ref, k_ref, vt_ref, lq1_ref, lk1_ref, lq2_ref, lk2_ref, on_ref, o_ref,
                 qm, s_buf, mx_buf, p_buf, a_buf, m_sc, l_sc, acc, *, lam_init):
    tq, tk = Q_TILE, KV_TILE
    i = pl.program_id(1)
    qt = qt_ref[...]
    feat = lax.broadcasted_iota(jnp.int32, qt.shape, 0)
    zero = jnp.zeros_like(qt)
    qm[:, 0:tq] = jnp.where(feat < QK_DIM, qt, zero)
    qm[:, tq:2 * tq] = jnp.where(feat >= QK_DIM, qt, zero)
    m_sc[...] = jnp.full(m_sc.shape, -jnp.inf, F32)
    l_sc[...] = jnp.zeros(l_sc.shape, F32)
    acc[...] = jnp.zeros(acc.shape, F32)

    def scores(tile, masked):
        start = pl.multiple_of(tile * tk, tk)
        st = _dot(k_ref[pl.ds(start, tk), :], qm[...])
        if masked:
            kpos = lax.broadcasted_iota(jnp.int32, st.shape, 0)
            qpos = lax.broadcasted_iota(jnp.int32, st.shape, 1)
            qpos = jnp.where(qpos >= tq, qpos - tq, qpos)
            st = jnp.where(kpos <= qpos, st, -jnp.inf)
        s_buf[...] = st
        mx_buf[...] = jnp.max(st, axis=0, keepdims=True)

    def softmax():
        m_prev = m_sc[...]
        m_new = jnp.maximum(m_prev, mx_buf[...])
        alpha = jnp.exp2(m_prev - m_new)
        p = jnp.exp2(s_buf[...] - m_new)
        l_sc[...] = alpha * l_sc[...] + jnp.sum(p, axis=0, keepdims=True)
        p_buf[...] = p.astype(BF16)
        a_buf[...] = alpha
        m_sc[...] = m_new

    def value(tile):
        start = pl.multiple_of(tile * tk, tk)
        acc[...] = acc[...] * a_buf[...] + _dot(vt_ref[:, pl.ds(start, tk)], p_buf[...])

    scores(i, masked=True)

    @pl.when(i == 0)
    def _():
        softmax()
        value(i)

    @pl.when(i > 0)
    def _():
        softmax()
        scores(0, masked=False)

        def body(n, carry):
            value(jnp.where(n == 2, i, n - 3))
            softmax()
            scores(n - 1, masked=False)
            return carry

        lax.fori_loop(2, i + 1, body, 0)
        value(jnp.where(i == 1, i, i - 2))
        softmax()
        value(i - 1)

    lam = (jnp.exp(jnp.sum(lq1_ref[...] * lk1_ref[...], axis=-1, keepdims=True))
           - jnp.exp(jnp.sum(lq2_ref[...] * lk2_ref[...], axis=-1, keepdims=True)) + lam_init)
    ot = acc[:, 0:tq] / l_sc[:, 0:tq] - lam * (acc[:, tq:2 * tq] / l_sc[:, tq:2 * tq])
    ot = ot * lax.rsqrt(jnp.mean(ot * ot, axis=0, keepdims=True) + EPS) * on_ref[...]
    o_ref[...] = (ot * (1.0 - lam_init)).T.astype(BF16)


def _resident(shape):
    nd = len(shape)
    return pl.BlockSpec(shape, lambda *_: (0,) * nd, pipeline_mode=pl.Buffered(1))


def _rows(width, tm=ROW_TILE):
    return pl.BlockSpec((tm, width), lambda i: (i, 0))


def _pre_call(x, g1, wg, wu, wd, gm, win, qn, kn, cw, cb, cn, ind):
    s = x.shape[0]
    row_major = pl.BlockSpec((N_HEADS, ROW_TILE, HEAD_DIM), lambda i: (0, i, 0))
    feat_major = pl.BlockSpec((N_HEADS, HEAD_DIM, ROW_TILE), lambda i: (0, 0, i))
    consts = (g1, wg, wu, wd, gm, win, qn, kn, cw, cb, cn, ind)
    return pl.pallas_call(
        _pre_kernel,
        grid=(s // ROW_TILE,),
        in_specs=[_rows(D_MODEL)] + [_resident(c.shape) for c in consts],
        out_specs=[_rows(D_MODEL), feat_major, row_major, feat_major, _rows(CONV_WIDTH)],
        out_shape=[jax.ShapeDtypeStruct((s, D_MODEL), F32),
                   jax.ShapeDtypeStruct((N_HEADS, HEAD_DIM, s), BF16),
                   jax.ShapeDtypeStruct((N_HEADS, s, HEAD_DIM), BF16),
                   jax.ShapeDtypeStruct((N_HEADS, HEAD_DIM, s), BF16),
                   jax.ShapeDtypeStruct((s, CONV_WIDTH), BF16)],
        scratch_shapes=[pltpu.VMEM((ROW_TILE + CARRY_ROWS, CONV_WIDTH), F32)],
        compiler_params=pltpu.CompilerParams(
            dimension_semantics=("arbitrary",), vmem_limit_bytes=VMEM_LIMIT_BYTES),
        name="pre_mix",
    )(x, *consts)


def _post_call(x1, o, yc, p, woa, wob, g2, wg, wu, wd, gp, wpg, wpp):
    s = x1.shape[0]
    consts = (woa, wob, g2, wg, wu, wd, gp, wpg, wpp)
    return pl.pallas_call(
        _post_kernel,
        grid=(s // ROW_TILE,),
        in_specs=[_rows(D_MODEL), _rows(ATTN_WIDTH), _rows(CONV_WIDTH), _rows(D_PLE)]
        + [_resident(c.shape) for c in consts],
        out_specs=_rows(D_MODEL),
        out_shape=jax.ShapeDtypeStruct((s, D_MODEL), F32),
        compiler_params=pltpu.CompilerParams(
            dimension_semantics=("parallel",), vmem_limit_bytes=VMEM_LIMIT_BYTES),
        name="post_mix",
    )(x1, o, yc, p, *consts)


def _attn_call(qt, k, vt, lq1, lk1, lq2, lk2, on, lam_init):
    s = k.shape[1]
    qt_spec = pl.BlockSpec((None, HEAD_DIM, Q_TILE), lambda h, i: (h, 0, i))
    k_spec = pl.BlockSpec((None, s, HEAD_DIM), lambda h, i: (h, 0, 0))
    vt_spec = pl.BlockSpec((None, HEAD_DIM, s), lambda h, i: (h, 0, 0))
    vec = pl.BlockSpec((1, QK_DIM), lambda h, i: (0, 0))
    return pl.pallas_call(
        functools.partial(_attn_kernel, lam_init=lam_init),
        grid=(N_HEADS, s // Q_TILE),
        in_specs=[qt_spec, k_spec, vt_spec, vec, vec, vec, vec,
                  pl.BlockSpec((HEAD_DIM, 1), lambda h, i: (0, 0))],
        out_specs=pl.BlockSpec((Q_TILE, HEAD_DIM), lambda h, i: (i, h)),
        out_shape=jax.ShapeDtypeStruct((s, ATTN_WIDTH), BF16),
        scratch_shapes=[pltpu.VMEM((HEAD_DIM, 2 * Q_TILE), BF16),
                        pltpu.VMEM((KV_TILE, 2 * Q_TILE), F32),
                        pltpu.VMEM((1, 2 * Q_TILE), F32),
                        pltpu.VMEM((KV_TILE, 2 * Q_TILE), BF16),
                        pltpu.VMEM((1, 2 * Q_TILE), F32),
                        pltpu.VMEM((1, 2 * Q_TILE), F32),
                        pltpu.VMEM((1, 2 * Q_TILE), F32),
                        pltpu.VMEM((HEAD_DIM, 2 * Q_TILE), F32)],
        compiler_params=pltpu.CompilerParams(
            dimension_semantics=("parallel", "parallel"), vmem_limit_bytes=VMEM_LIMIT_BYTES),
        name="diff_attn",
    )(qt, k, vt, lq1, lk1, lq2, lk2, on)


def kernel(x, p, ffn1_norm, ffn1_w_gate, ffn1_w_up, ffn1_w_down, mix_norm, w_in, q_norm, k_norm,
           lambda_q1, lambda_k1, lambda_q2, lambda_k2, attn_out_norm, conv_w, conv_b, conv_out_norm,
           w_out, ffn2_norm, ffn2_w_gate, ffn2_w_up, ffn2_w_down, ple_norm, ple_w_gate, ple_w_proj):
    b, s, _ = x.shape
    depth = p.shape[0]
    assert b == 1 and s % Q_TILE == 0 and s % ROW_TILE == 0 and Q_TILE == KV_TILE
    group = jnp.arange(ATTN_WIDTH, dtype=jnp.int32) // GROUP_DIM
    ind = (group[:, None] == group[None, :]).astype(BF16)

    def row(a):
        return a.reshape(1, -1).astype(F32)

    def tiled(a):
        return jnp.tile(a.astype(F32), ATTN_WIDTH // a.shape[0]).reshape(1, ATTN_WIDTH)

    xs = x.reshape(s, D_MODEL)
    for i in range(depth):
        lam_init = 0.8 - 0.6 * math.exp(-0.3 * i)
        x1, qt, k, vt, yc = _pre_call(
            xs, row(ffn1_norm[i]), ffn1_w_gate[i].astype(BF16), ffn1_w_up[i].astype(BF16),
            ffn1_w_down[i].astype(BF16), row(mix_norm[i]), w_in[i].astype(BF16),
            tiled(q_norm[i]), tiled(k_norm[i]), conv_w[i].astype(F32), row(conv_b[i]),
            row(conv_out_norm[i]), ind)
        o = _attn_call(qt, k, vt, row(lambda_q1[i]), row(lambda_k1[i]), row(lambda_q2[i]),
                       row(lambda_k2[i]), attn_out_norm[i].astype(F32).reshape(HEAD_DIM, 1), lam_init)
        w_o = w_out[i].astype(BF16)
        xs = _post_call(
            x1, o, yc, p[i].reshape(s, D_PLE), w_o[:ATTN_WIDTH], w_o[ATTN_WIDTH:],
            row(ffn2_norm[i]), ffn2_w_gate[i].astype(BF16), ffn2_w_up[i].astype(BF16),
            ffn2_w_down[i].astype(BF16), row(ple_norm[i]), ple_w_gate[i].astype(BF16),
            ple_w_proj[i].astype(BF16))
    return xs.reshape(b, s, D_MODEL)
```

```python
import functools
import math

import jax
import jax.numpy as jnp
from jax import lax
from jax.experimental import pallas as pl
from jax.experimental.pallas import tpu as pltpu

D_MODEL = 1024
D_FF = 2816
D_PLE = 256
ATTN_WIDTH = 512
CONV_WIDTH = 512
N_HEADS = 4
HEAD_DIM = 128
QK_DIM = 64
GROUP_DIM = 64
CONV_K = 3
EPS = 1e-6

ROW_TILE = 256
Q_TILE = 1024
KV_TILE = 1024
CARRY_ROWS = 8
VMEM_LIMIT_BYTES = 56 * 1024 * 1024

BF16 = jnp.bfloat16
F32 = jnp.float32


def _dot(a, b):
    return jnp.dot(a, b, preferred_element_type=F32)


def _rms(x, g):
    return x * lax.rsqrt(jnp.mean(x * x, axis=-1, keepdims=True) + EPS) * g


def _group_rms(x, g, ind):
    sq = x * x
    hi = sq.astype(BF16)
    lo = (sq - hi.astype(F32)).astype(BF16)
    ms = (_dot(hi, ind) + _dot(lo, ind)) * (1.0 / GROUP_DIM)
    return x * lax.rsqrt(ms + EPS) * g


def _swiglu_half_step(x, g_ref, wg_ref, wu_ref, wd_ref):
    n = _rms(x, g_ref[...]).astype(BF16)
    gate = _dot(n, wg_ref[...])
    up = _dot(n, wu_ref[...])
    h = (gate * jax.nn.sigmoid(gate) * up).astype(BF16)
    return x + 0.5 * _dot(h, wd_ref[...])


def _pre_kernel(x_ref, g1_ref, wg_ref, wu_ref, wd_ref, gm_ref, win_ref, qn_ref, kn_ref,
                cw_ref, cb_ref, cn_ref, ind_ref,
                x1_ref, q_ref, k_ref, v_ref, yc_ref, u_ext):
    tm = x_ref.shape[0]
    x1 = _swiglu_half_step(x_ref[...], g1_ref, wg_ref, wu_ref, wd_ref)
    x1_ref[...] = x1
    h = _rms(x1, gm_ref[...]).astype(BF16)
    ind = ind_ref[...]

    def z_cols(section):
        return _dot(h, win_ref[:, section * ATTN_WIDTH:(section + 1) * ATTN_WIDTH])

    q = _group_rms(z_cols(0), qn_ref[...], ind) * (math.log2(math.e) / math.sqrt(QK_DIM))
    k = _group_rms(z_cols(1), kn_ref[...], ind)
    v = z_cols(2)
    for hd in range(N_HEADS):
        cols = slice(hd * HEAD_DIM, (hd + 1) * HEAD_DIM)
        q_ref[hd] = q[:, cols].T.astype(BF16)
        k_ref[hd] = k[:, cols].astype(BF16)
        v_ref[hd] = v[:, cols].T.astype(BF16)

    @pl.when(pl.program_id(0) == 0)
    def _():
        u_ext[0:CARRY_ROWS, :] = jnp.zeros((CARRY_ROWS, CONV_WIDTH), F32)

    u = z_cols(4) * z_cols(5)
    u_ext[CARRY_ROWS:CARRY_ROWS + tm, :] = u
    conv = (cw_ref[0:1, :] * u_ext[pl.ds(CARRY_ROWS - 2, tm), :]
            + cw_ref[1:2, :] * u_ext[pl.ds(CARRY_ROWS - 1, tm), :]
            + cw_ref[2:3, :] * u + cb_ref[...])
    u_ext[0:CARRY_ROWS, :] = u_ext[tm:tm + CARRY_ROWS, :]
    yc = z_cols(3) * conv
    yc_ref[...] = _group_rms(yc, cn_ref[...], ind).astype(BF16)


def _post_kernel(x1_ref, o_ref, yc_ref, p_ref, woa_ref, wob_ref, g2_ref, wg_ref, wu_ref, wd_ref,
                 gp_ref, wpg_ref, wpp_ref, out_ref):
    x2 = x1_ref[...] + _dot(o_ref[...], woa_ref[...]) + _dot(yc_ref[...], wob_ref[...])
    x3 = _swiglu_half_step(x2, g2_ref, wg_ref, wu_ref, wd_ref)
    gate = jax.nn.sigmoid(_dot(_rms(x3, gp_ref[...]).astype(BF16), wpg_ref[...]))
    out_ref[...] = x3 + gate * _dot(p_ref[...].astype(BF16), wpp_ref[...])


def _attn_kernel(qt_ref, k_ref, vt_ref, lq1_ref, lk1_ref, lq2_ref, lk2_ref, on_ref, o_ref,
                 qm, s_buf, mx_buf, p_buf, a_buf, m_sc, l_sc, acc, *, lam_init):
    tq, tk = Q_TILE, KV_TILE
    i = pl.program_id(1)
    qt = qt_ref[...]
    feat = lax.broadcasted_iota(jnp.int32, qt.shape, 0)
    zero = jnp.zeros_like(qt)
    qm[:, 0:tq] = jnp.where(feat < QK_DIM, qt, zero)
    qm[:, tq:2 * tq] = jnp.where(feat >= QK_DIM, qt, zero)
    m_sc[...] = jnp.full(m_sc.shape, -jnp.inf, F32)
    l_sc[...] = jnp.zeros(l_sc.shape, F32)
    acc[...] = jnp.zeros(acc.shape, F32)

    def scores(tile, masked):
        start = pl.multiple_of(tile * tk, tk)
        st = _dot(k_ref[pl.ds(start, tk), :], qm[...])
        if masked:
            kpos = lax.broadcasted_iota(jnp.int32, st.shape, 0)
            qpos = lax.broadcasted_iota(jnp.int32, st.shape, 1)
            qpos = jnp.where(qpos >= tq, qpos - tq, qpos)
            st = jnp.where(kpos <= qpos, st, -jnp.inf)
        s_buf[...] = st
        mx_buf[...] = jnp.max(st, axis=0, keepdims=True)

    def softmax():
        m_prev = m_sc[...]
        m_new = jnp.maximum(m_prev, mx_buf[...])
        alpha = jnp.exp2(m_prev - m_new)
        p = jnp.exp2(s_buf[...] - m_new)
        l_sc[...] = alpha * l_sc[...] + jnp.sum(p, axis=0, keepdims=True)
        p_buf[...] = p.astype(BF16)
        a_buf[...] = alpha
        m_sc[...] = m_new

    def value(tile):
        start = pl.multiple_of(tile * tk, tk)
        acc[...] = acc[...] * a_buf[...] + _dot(vt_ref[:, pl.ds(start, tk)], p_buf[...])

    scores(i, masked=True)

    @pl.when(i == 0)
    def _():
        softmax()
        value(i)

    @pl.when(i > 0)
    def _():
        softmax()
        scores(0, masked=False)

        def body(n, carry):
            value(jnp.where(n == 2, i, n - 3))
            softmax()
            scores(n - 1, masked=False)
            return carry

        lax.fori_loop(2, i + 1, body, 0)
        value(jnp.where(i == 1, i, i - 2))
        softmax()
        value(i - 1)

    lam = (jnp.exp(jnp.sum(lq1_ref[...] * lk1_ref[...], axis=-1, keepdims=True))
           - jnp.exp(jnp.sum(lq2_ref[...] * lk2_ref[...], axis=-1, keepdims=True)) + lam_init)
    ot = acc[:, 0:tq] / l_sc[:, 0:tq] - lam * (acc[:, tq:2 * tq] / l_sc[:, tq:2 * tq])
    ot = ot * lax.rsqrt(jnp.mean(ot * ot, axis=0, keepdims=True) + EPS) * on_ref[...]
    o_ref[...] = (ot * (1.0 - lam_init)).T.astype(BF16)


def _resident(shape):
    nd = len(shape)
    return pl.BlockSpec(shape, lambda *_: (0,) * nd, pipeline_mode=pl.Buffered(1))


def _rows(width, tm=ROW_TILE):
    return pl.BlockSpec((tm, width), lambda i: (i, 0))


def _pre_call(x, g1, wg, wu, wd, gm, win, qn, kn, cw, cb, cn, ind):
    s = x.shape[0]
    row_major = pl.BlockSpec((N_HEADS, ROW_TILE, HEAD_DIM), lambda i: (0, i, 0))
    feat_major = pl.BlockSpec((N_HEADS, HEAD_DIM, ROW_TILE), lambda i: (0, 0, i))
    consts = (g1, wg, wu, wd, gm, win, qn, kn, cw, cb, cn, ind)
    return pl.pallas_call(
        _pre_kernel,
        grid=(s // ROW_TILE,),
        in_specs=[_rows(D_MODEL)] + [_resident(c.shape) for c in consts],
        out_specs=[_rows(D_MODEL), feat_major, row_major, feat_major, _rows(CONV_WIDTH)],
        out_shape=[jax.ShapeDtypeStruct((s, D_MODEL), F32),
                   jax.ShapeDtypeStruct((N_HEADS, HEAD_DIM, s), BF16),
                   jax.ShapeDtypeStruct((N_HEADS, s, HEAD_DIM), BF16),
                   jax.ShapeDtypeStruct((N_HEADS, HEAD_DIM, s), BF16),
                   jax.ShapeDtypeStruct((s, CONV_WIDTH), BF16)],
        scratch_shapes=[pltpu.VMEM((ROW_TILE + CARRY_ROWS, CONV_WIDTH), F32)],
        compiler_params=pltpu.CompilerParams(
            dimension_semantics=("arbitrary",), vmem_limit_bytes=VMEM_LIMIT_BYTES),
        name="pre_mix",
    )(x, *consts)


def _post_call(x1, o, yc, p, woa, wob, g2, wg, wu, wd, gp, wpg, wpp):
    s = x1.shape[0]
    consts = (woa, wob, g2, wg, wu, wd, gp, wpg, wpp)
    return pl.pallas_call(
        _post_kernel,
        grid=(s // ROW_TILE,),
        in_specs=[_rows(D_MODEL), _rows(ATTN_WIDTH), _rows(CONV_WIDTH), _rows(D_PLE)]
        + [_resident(c.shape) for c in consts],
        out_specs=_rows(D_MODEL),
        out_shape=jax.ShapeDtypeStruct((s, D_MODEL), F32),
        compiler_params=pltpu.CompilerParams(
            dimension_semantics=("parallel",), vmem_limit_bytes=VMEM_LIMIT_BYTES),
        name="post_mix",
    )(x1, o, yc, p, *consts)


def _attn_call(qt, k, vt, lq1, lk1, lq2, lk2, on, lam_init):
    s = k.shape[1]
    qt_spec = pl.BlockSpec((None, HEAD_DIM, Q_TILE), lambda h, i: (h, 0, i))
    k_spec = pl.BlockSpec((None, s, HEAD_DIM), lambda h, i: (h, 0, 0))
    vt_spec = pl.BlockSpec((None, HEAD_DIM, s), lambda h, i: (h, 0, 0))
    vec = pl.BlockSpec((1, QK_DIM), lambda h, i: (0, 0))
    return pl.pallas_call(
        functools.partial(_attn_kernel, lam_init=lam_init),
        grid=(N_HEADS, s // Q_TILE),
        in_specs=[qt_spec, k_spec, vt_spec, vec, vec, vec, vec,
                  pl.BlockSpec((HEAD_DIM, 1), lambda h, i: (0, 0))],
        out_specs=pl.BlockSpec((Q_TILE, HEAD_DIM), lambda h, i: (i, h)),
        out_shape=jax.ShapeDtypeStruct((s, ATTN_WIDTH), BF16),
        scratch_shapes=[pltpu.VMEM((HEAD_DIM, 2 * Q_TILE), BF16),
                        pltpu.VMEM((KV_TILE, 2 * Q_TILE), F32),
                        pltpu.VMEM((1, 2 * Q_TILE), F32),
                        pltpu.VMEM((KV_TILE, 2 * Q_TILE), BF16),
                        pltpu.VMEM((1, 2 * Q_TILE), F32),
                        pltpu.VMEM((1, 2 * Q_TILE), F32),
                        pltpu.VMEM((1, 2 * Q_TILE), F32),
                        pltpu.VMEM((HEAD_DIM, 2 * Q_TILE), F32)],
        compiler_params=pltpu.CompilerParams(
            dimension_semantics=("parallel", "parallel"), vmem_limit_bytes=VMEM_LIMIT_BYTES),
        name="diff_attn",
    )(qt, k, vt, lq1, lk1, lq2, lk2, on)


def kernel(x, p, ffn1_norm, ffn1_w_gate, ffn1_w_up, ffn1_w_down, mix_norm, w_in, q_norm, k_norm,
           lambda_q1, lambda_k1, lambda_q2, lambda_k2, attn_out_norm, conv_w, conv_b, conv_out_norm,
           w_out, ffn2_norm, ffn2_w_gate, ffn2_w_up, ffn2_w_down, ple_norm, ple_w_gate, ple_w_proj):
    b, s, _ = x.shape
    depth = p.shape[0]
    assert b == 1 and s % Q_TILE == 0 and s % ROW_TILE == 0 and Q_TILE == KV_TILE
    group = jnp.arange(ATTN_WIDTH, dtype=jnp.int32) // GROUP_DIM
    ind = (group[:, None] == group[None, :]).astype(BF16)

    def row(a):
        return a.reshape(1, -1).astype(F32)

    def tiled(a):
        return jnp.tile(a.astype(F32), ATTN_WIDTH // a.shape[0]).reshape(1, ATTN_WIDTH)

    xs = x.reshape(s, D_MODEL)
    for i in range(depth):
        lam_init = 0.8 - 0.6 * math.exp(-0.3 * i)
        x1, qt, k, vt, yc = _pre_call(
            xs, row(ffn1_norm[i]), ffn1_w_gate[i].astype(BF16), ffn1_w_up[i].astype(BF16),
            ffn1_w_down[i].astype(BF16), row(mix_norm[i]), w_in[i].astype(BF16),
            tiled(q_norm[i]), tiled(k_norm[i]), conv_w[i].astype(F32), row(conv_b[i]),
            row(conv_out_norm[i]), ind)
        o = _attn_call(qt, k, vt, row(lambda_q1[i]), row(lambda_k1[i]), row(lambda_q2[i]),
                       row(lambda_k2[i]), attn_out_norm[i].astype(F32).reshape(HEAD_DIM, 1), lam_init)
        w_o = w_out[i].astype(BF16)
        xs = _post_call(
            x1, o, yc, p[i].reshape(s, D_PLE), w_o[:ATTN_WIDTH], w_o[ATTN_WIDTH:],
            row(ffn2_norm[i]), ffn2_w_gate[i].astype(BF16), ffn2_w_up[i].astype(BF16),
            ffn2_w_down[i].astype(BF16), row(ple_norm[i]), ple_w_gate[i].astype(BF16),
            ple_w_proj[i].astype(BF16))
    return xs.reshape(b, s, D_MODEL)
```

```python
import functools
import math

import jax
import jax.numpy as jnp
from jax import lax
from jax.experimental import pallas as pl
from jax.experimental.pallas import tpu as pltpu

D_MODEL = 1024
D_FF = 2816
D_PLE = 256
ATTN_WIDTH = 512
CONV_WIDTH = 512
N_HEADS = 4
HEAD_DIM = 128
QK_DIM = 64
GROUP_DIM = 64
CONV_K = 3
EPS = 1e-6

ROW_TILE = 256
Q_TILE = 1024
KV_TILE = 1024
MXU_TILE = 256
LANE_BLOCK = MXU_TILE
CARRY_ROWS = 8
VMEM_LIMIT_BYTES = 56 * 1024 * 1024

BF16 = jnp.bfloat16
F32 = jnp.float32


def _dot(a, b):
    return jnp.dot(a, b, preferred_element_type=F32)


def _rms(x, g):
    return x * lax.rsqrt(jnp.mean(x * x, axis=-1, keepdims=True) + EPS) * g


def _group_rms(x, g, ind):
    sq = x * x
    hi = sq.astype(BF16)
    lo = (sq - hi.astype(F32)).astype(BF16)
    sums = [_dot(hi[:, c:c + MXU_TILE], ind) + _dot(lo[:, c:c + MXU_TILE], ind)
            for c in range(0, x.shape[1], MXU_TILE)]
    ms = jnp.concatenate(sums, axis=1) * (1.0 / GROUP_DIM)
    return x * lax.rsqrt(ms + EPS) * g


def _swiglu_half_step(x, g_ref, wg_ref, wu_ref, wd_ref):
    n = _rms(x, g_ref[...]).astype(BF16)
    gate = _dot(n, wg_ref[...])
    up = _dot(n, wu_ref[...])
    h = (gate * jax.nn.sigmoid(gate) * up).astype(BF16)
    return x + 0.5 * _dot(h, wd_ref[...])


def _pre_kernel(x_ref, g1_ref, wg_ref, wu_ref, wd_ref, gm_ref, win_ref, qn_ref, kn_ref,
                cw_ref, cb_ref, cn_ref, ind_ref,
                x1_ref, q_ref, k_ref, v_ref, yc_ref, u_ext):
    tm = x_ref.shape[0]
    x1 = _swiglu_half_step(x_ref[...], g1_ref, wg_ref, wu_ref, wd_ref)
    x1_ref[...] = x1
    h = _rms(x1, gm_ref[...]).astype(BF16)
    ind = ind_ref[...]

    def z_cols(section):
        return _dot(h, win_ref[:, section * ATTN_WIDTH:(section + 1) * ATTN_WIDTH])

    q = _group_rms(z_cols(0), qn_ref[...], ind) * (math.log2(math.e) / math.sqrt(QK_DIM))
    k = _group_rms(z_cols(1), kn_ref[...], ind)
    v = z_cols(2)
    for hd in range(N_HEADS):
        cols = slice(hd * HEAD_DIM, (hd + 1) * HEAD_DIM)
        q_ref[hd] = q[:, cols].T.astype(BF16)
        k_ref[hd] = k[:, cols].astype(BF16)
        v_ref[hd] = v[:, cols].T.astype(BF16)

    @pl.when(pl.program_id(0) == 0)
    def _():
        u_ext[0:CARRY_ROWS, :] = jnp.zeros((CARRY_ROWS, CONV_WIDTH), F32)

    u = z_cols(4) * z_cols(5)
    u_ext[CARRY_ROWS:CARRY_ROWS + tm, :] = u
    conv = (cw_ref[0:1, :] * u_ext[pl.ds(CARRY_ROWS - 2, tm), :]
            + cw_ref[1:2, :] * u_ext[pl.ds(CARRY_ROWS - 1, tm), :]
            + cw_ref[2:3, :] * u + cb_ref[...])
    u_ext[0:CARRY_ROWS, :] = u_ext[tm:tm + CARRY_ROWS, :]
    yc = z_cols(3) * conv
    yc_ref[...] = _group_rms(yc, cn_ref[...], ind).astype(BF16)


def _post_kernel(x1_ref, o_ref, yc_ref, p_ref, woa_ref, wob_ref, g2_ref, wg_ref, wu_ref, wd_ref,
                 gp_ref, wpg_ref, wpp_ref, out_ref):
    x2 = x1_ref[...] + _dot(o_ref[...], woa_ref[...]) + _dot(yc_ref[...], wob_ref[...])
    x3 = _swiglu_half_step(x2, g2_ref, wg_ref, wu_ref, wd_ref)
    gate = jax.nn.sigmoid(_dot(_rms(x3, gp_ref[...]).astype(BF16), wpg_ref[...]))
    out_ref[...] = x3 + gate * _dot(p_ref[...].astype(BF16), wpp_ref[...])


def _attn_kernel(qt_ref, k_ref, vt_ref, lq1_ref, lk1_ref, lq2_ref, lk2_ref, on_ref, o_ref,
                 qm, s_buf, mx_buf, p_buf, a_buf, m_sc, l_sc, acc, *, lam_init):
    tq, tk = Q_TILE, KV_TILE
    i = pl.program_id(1)
    qt = qt_ref[...]
    feat = lax.broadcasted_iota(jnp.int32, qt.shape, 0)
    zero = jnp.zeros_like(qt)
    qm[:, 0:tq] = jnp.where(feat < QK_DIM, qt, zero)
    qm[:, tq:2 * tq] = jnp.where(feat >= QK_DIM, qt, zero)
    m_sc[...] = jnp.full(m_sc.shape, -jnp.inf, F32)
    l_sc[...] = jnp.zeros(l_sc.shape, F32)
    acc[...] = jnp.zeros(acc.shape, F32)

    def scores(tile, lanes, masked):
        start = pl.multiple_of(tile * tk, tk)
        st = _dot(k_ref[pl.ds(start, tk), :], qm[:, lanes])
        if masked:
            kpos = lax.broadcasted_iota(jnp.int32, st.shape, 0)
            qpos = lax.broadcasted_iota(jnp.int32, st.shape, 1) + (lanes.start % tq)
            st = jnp.where(kpos <= qpos, st, -jnp.inf)
        s_buf[:, lanes] = st
        mx_buf[:, lanes] = jnp.max(st, axis=0, keepdims=True)

    def softmax(lanes):
        m_prev = m_sc[:, lanes]
        m_new = jnp.maximum(m_prev, mx_buf[:, lanes])
        alpha = jnp.exp2(m_prev - m_new)
        p = jnp.exp2(s_buf[:, lanes] - m_new)
        l_sc[:, lanes] = alpha * l_sc[:, lanes] + jnp.sum(p, axis=0, keepdims=True)
        p_buf[:, lanes] = p.astype(BF16)
        a_buf[:, lanes] = alpha
        m_sc[:, lanes] = m_new

    def value(tile, lanes):
        start = pl.multiple_of(tile * tk, tk)
        acc[:, lanes] = (acc[:, lanes] * a_buf[:, lanes]
                         + _dot(vt_ref[:, pl.ds(start, tk)], p_buf[:, lanes]))

    def step(value_tile=None, do_softmax=False, scores_tile=None, masked=False):
        for blk in range(2 * tq // LANE_BLOCK):
            lanes = slice(blk * LANE_BLOCK, (blk + 1) * LANE_BLOCK)
            if value_tile is not None:
                value(value_tile, lanes)
            if do_softmax:
                softmax(lanes)
            if scores_tile is not None:
                scores(scores_tile, lanes, masked)

    step(scores_tile=i, masked=True)

    @pl.when(i == 0)
    def _():
        step(do_softmax=True)
        step(value_tile=i)

    @pl.when(i > 0)
    def _():
        step(do_softmax=True, scores_tile=0)

        def body(n, carry):
            step(value_tile=jnp.where(n == 2, i, n - 3), do_softmax=True, scores_tile=n - 1)
            return carry

        lax.fori_loop(2, i + 1, body, 0)
        step(value_tile=jnp.where(i == 1, i, i - 2), do_softmax=True)
        step(value_tile=i - 1)

    lam = (jnp.exp(jnp.sum(lq1_ref[...] * lk1_ref[...], axis=-1, keepdims=True))
           - jnp.exp(jnp.sum(lq2_ref[...] * lk2_ref[...], axis=-1, keepdims=True)) + lam_init)
    ot = acc[:, 0:tq] / l_sc[:, 0:tq] - lam * (acc[:, tq:2 * tq] / l_sc[:, tq:2 * tq])
    ot = ot * lax.rsqrt(jnp.mean(ot * ot, axis=0, keepdims=True) + EPS) * on_ref[...]
    o_ref[...] = (ot * (1.0 - lam_init)).T.astype(BF16)


def _resident(a):
    if isinstance(a, tuple):
        _, block_shape, block_index = a
        return pl.BlockSpec(block_shape, lambda *_: block_index, pipeline_mode=pl.Buffered(1))
    return pl.BlockSpec(a.shape, lambda *_: (0,) * a.ndim, pipeline_mode=pl.Buffered(1))


def _operand(a):
    return a[0] if isinstance(a, tuple) else a


def _layer(stacked, layer):
    return (stacked, (None,) + stacked.shape[1:], (layer, 0, 0))


def _rows(width, tm=ROW_TILE):
    return pl.BlockSpec((tm, width), lambda i: (i, 0))


def _pre_call(x, g1, wg, wu, wd, gm, win, qn, kn, cw, cb, cn, ind):
    s = x.shape[0]
    row_major = pl.BlockSpec((N_HEADS, ROW_TILE, HEAD_DIM), lambda i: (0, i, 0))
    feat_major = pl.BlockSpec((N_HEADS, HEAD_DIM, ROW_TILE), lambda i: (0, 0, i))
    consts = (g1, wg, wu, wd, gm, win, qn, kn, cw, cb, cn, ind)
    return pl.pallas_call(
        _pre_kernel,
        grid=(s // ROW_TILE,),
        in_specs=[_rows(D_MODEL)] + [_resident(c) for c in consts],
        out_specs=[_rows(D_MODEL), feat_major, row_major, feat_major, _rows(CONV_WIDTH)],
        out_shape=[jax.ShapeDtypeStruct((s, D_MODEL), F32),
                   jax.ShapeDtypeStruct((N_HEADS, HEAD_DIM, s), BF16),
                   jax.ShapeDtypeStruct((N_HEADS, s, HEAD_DIM), BF16),
                   jax.ShapeDtypeStruct((N_HEADS, HEAD_DIM, s), BF16),
                   jax.ShapeDtypeStruct((s, CONV_WIDTH), BF16)],
        scratch_shapes=[pltpu.VMEM((ROW_TILE + CARRY_ROWS, CONV_WIDTH), F32)],
        compiler_params=pltpu.CompilerParams(
            dimension_semantics=("arbitrary",), vmem_limit_bytes=VMEM_LIMIT_BYTES),
        name="pre_mix",
    )(x, *map(_operand, consts))


def _post_call(x1, o, yc, p, layer, woa, wob, g2, wg, wu, wd, gp, wpg, wpp):
    s = x1.shape[0]
    consts = (woa, wob, g2, wg, wu, wd, gp, wpg, wpp)
    p_rows = pl.BlockSpec((None, None, ROW_TILE, D_PLE), lambda i: (layer, 0, i, 0))
    return pl.pallas_call(
        _post_kernel,
        grid=(s // ROW_TILE,),
        in_specs=[_rows(D_MODEL), _rows(ATTN_WIDTH), _rows(CONV_WIDTH), p_rows]
        + [_resident(c) for c in consts],
        out_specs=_rows(D_MODEL),
        out_shape=jax.ShapeDtypeStruct((s, D_MODEL), F32),
        compiler_params=pltpu.CompilerParams(
            dimension_semantics=("parallel",), vmem_limit_bytes=VMEM_LIMIT_BYTES),
        name="post_mix",
    )(x1, o, yc, p, *map(_operand, consts))


def _attn_call(qt, k, vt, lq1, lk1, lq2, lk2, on, lam_init):
    s = k.shape[1]
    qt_spec = pl.BlockSpec((None, HEAD_DIM, Q_TILE), lambda h, i: (h, 0, i))
    single = pl.Buffered(1)
    k_spec = pl.BlockSpec((None, s, HEAD_DIM), lambda h, i: (h, 0, 0), pipeline_mode=single)
    vt_spec = pl.BlockSpec((None, HEAD_DIM, s), lambda h, i: (h, 0, 0), pipeline_mode=single)
    vec = pl.BlockSpec((1, QK_DIM), lambda h, i: (0, 0))
    return pl.pallas_call(
        functools.partial(_attn_kernel, lam_init=lam_init),
        grid=(N_HEADS, s // Q_TILE),
        in_specs=[qt_spec, k_spec, vt_spec, vec, vec, vec, vec,
                  pl.BlockSpec((HEAD_DIM, 1), lambda h, i: (0, 0))],
        out_specs=pl.BlockSpec((Q_TILE, HEAD_DIM), lambda h, i: (i, h)),
        out_shape=jax.ShapeDtypeStruct((s, ATTN_WIDTH), BF16),
        scratch_shapes=[pltpu.VMEM((HEAD_DIM, 2 * Q_TILE), BF16),
                        pltpu.VMEM((KV_TILE, 2 * Q_TILE), F32),
                        pltpu.VMEM((1, 2 * Q_TILE), F32),
                        pltpu.VMEM((KV_TILE, 2 * Q_TILE), BF16),
                        pltpu.VMEM((1, 2 * Q_TILE), F32),
                        pltpu.VMEM((1, 2 * Q_TILE), F32),
                        pltpu.VMEM((1, 2 * Q_TILE), F32),
                        pltpu.VMEM((HEAD_DIM, 2 * Q_TILE), F32)],
        compiler_params=pltpu.CompilerParams(
            dimension_semantics=("parallel", "parallel"), vmem_limit_bytes=VMEM_LIMIT_BYTES),
        name="diff_attn",
    )(qt, k, vt, lq1, lk1, lq2, lk2, on)


def kernel(x, p, ffn1_norm, ffn1_w_gate, ffn1_w_up, ffn1_w_down, mix_norm, w_in, q_norm, k_norm,
           lambda_q1, lambda_k1, lambda_q2, lambda_k2, attn_out_norm, conv_w, conv_b, conv_out_norm,
           w_out, ffn2_norm, ffn2_w_gate, ffn2_w_up, ffn2_w_down, ple_norm, ple_w_gate, ple_w_proj):
    b, s, _ = x.shape
    depth = p.shape[0]
    assert b == 1 and s % Q_TILE == 0 and s % ROW_TILE == 0 and Q_TILE == KV_TILE
    group = jnp.arange(MXU_TILE, dtype=jnp.int32) // GROUP_DIM
    ind = (group[:, None] == group[None, :]).astype(BF16)

    def row(a):
        return a.reshape(1, -1).astype(F32)

    def tiled(a):
        return jnp.tile(a.astype(F32), ATTN_WIDTH // a.shape[0]).reshape(1, ATTN_WIDTH)

    wg1, wu1, wd1, win, wo, wg2, wu2, wd2, wpg, wpp = (
        w.astype(BF16) for w in (ffn1_w_gate, ffn1_w_up, ffn1_w_down, w_in, w_out,
                                 ffn2_w_gate, ffn2_w_up, ffn2_w_down, ple_w_gate, ple_w_proj))
    xs = x.reshape(s, D_MODEL)
    for i in range(depth):
        lam_init = 0.8 - 0.6 * math.exp(-0.3 * i)
        x1, qt, k, vt, yc = _pre_call(
            xs, row(ffn1_norm[i]), _layer(wg1, i), _layer(wu1, i), _layer(wd1, i), row(mix_norm[i]),
            _layer(win, i), tiled(q_norm[i]), tiled(k_norm[i]), conv_w[i].astype(F32),
            row(conv_b[i]), row(conv_out_norm[i]), ind)
        o = _attn_call(qt, k, vt, row(lambda_q1[i]), row(lambda_k1[i]), row(lambda_q2[i]),
                       row(lambda_k2[i]), attn_out_norm[i].astype(F32).reshape(HEAD_DIM, 1), lam_init)
        wo_half = (None, ATTN_WIDTH, D_MODEL)
        xs = _post_call(
            x1, o, yc, p, i, (wo, wo_half, (i, 0, 0)), (wo, wo_half, (i, 1, 0)),
            row(ffn2_norm[i]), _layer(wg2, i), _layer(wu2, i), _layer(wd2, i), row(ple_norm[i]),
            _layer(wpg, i), _layer(wpp, i))
    return xs.reshape(b, s, D_MODEL)
```

```python
import functools
import math

import jax
import jax.numpy as jnp
from jax import lax
from jax.experimental import pallas as pl
from jax.experimental.pallas import tpu as pltpu

D_MODEL = 1024
D_FF = 2816
D_PLE = 256
ATTN_WIDTH = 512
CONV_WIDTH = 512
N_HEADS = 4
HEAD_DIM = 128
QK_DIM = 64
GROUP_DIM = 64
CONV_K = 3
EPS = 1e-6

ROW_TILE = 512
Q_TILE = 1024
KV_TILE = 1024
MXU_TILE = 256
LANE_BLOCK = MXU_TILE
CARRY_ROWS = 8
VMEM_LIMIT_BYTES = 56 * 1024 * 1024

BF16 = jnp.bfloat16
F32 = jnp.float32


def _dot(a, b):
    return jnp.dot(a, b, preferred_element_type=F32)


def _rms(x, g):
    return x * lax.rsqrt(jnp.mean(x * x, axis=-1, keepdims=True) + EPS) * g


def _group_rms(x, g, ind):
    sq = x * x
    hi = sq.astype(BF16)
    lo = (sq - hi.astype(F32)).astype(BF16)
    sums = [_dot(hi[:, c:c + MXU_TILE], ind) + _dot(lo[:, c:c + MXU_TILE], ind)
            for c in range(0, x.shape[1], MXU_TILE)]
    ms = jnp.concatenate(sums, axis=1) * (1.0 / GROUP_DIM)
    return x * lax.rsqrt(ms + EPS) * g


def _swiglu_half_step(x, g_ref, wg_ref, wu_ref, wd_ref):
    n = _rms(x, g_ref[...]).astype(BF16)
    gate = _dot(n, wg_ref[...])
    up = _dot(n, wu_ref[...])
    h = (gate * jax.nn.sigmoid(gate) * up).astype(BF16)
    return x + 0.5 * _dot(h, wd_ref[...])


def _pre_kernel(x_ref, g1_ref, wg_ref, wu_ref, wd_ref, gm_ref, win_ref, qn_ref, kn_ref,
                cw_ref, cb_ref, cn_ref, ind_ref,
                x1_ref, q_ref, k_ref, v_ref, yc_ref, u_ext):
    tm = x_ref.shape[0]
    x1 = _swiglu_half_step(x_ref[...], g1_ref, wg_ref, wu_ref, wd_ref)
    x1_ref[...] = x1
    h = _rms(x1, gm_ref[...]).astype(BF16)
    ind = ind_ref[...]

    def z_cols(section):
        return _dot(h, win_ref[:, section * ATTN_WIDTH:(section + 1) * ATTN_WIDTH])

    q = _group_rms(z_cols(0), qn_ref[...], ind) * (math.log2(math.e) / math.sqrt(QK_DIM))
    k = _group_rms(z_cols(1), kn_ref[...], ind)
    v = z_cols(2)
    for hd in range(N_HEADS):
        cols = slice(hd * HEAD_DIM, (hd + 1) * HEAD_DIM)
        q_ref[hd] = q[:, cols].T.astype(BF16)
        k_ref[hd] = k[:, cols].astype(BF16)
        v_ref[hd] = v[:, cols].T.astype(BF16)

    @pl.when(pl.program_id(0) == 0)
    def _():
        u_ext[0:CARRY_ROWS, :] = jnp.zeros((CARRY_ROWS, CONV_WIDTH), F32)

    u = z_cols(4) * z_cols(5)
    u_ext[CARRY_ROWS:CARRY_ROWS + tm, :] = u
    conv = (cw_ref[0:1, :] * u_ext[pl.ds(CARRY_ROWS - 2, tm), :]
            + cw_ref[1:2, :] * u_ext[pl.ds(CARRY_ROWS - 1, tm), :]
            + cw_ref[2:3, :] * u + cb_ref[...])
    u_ext[0:CARRY_ROWS, :] = u_ext[tm:tm + CARRY_ROWS, :]
    yc = z_cols(3) * conv
    yc_ref[...] = _group_rms(yc, cn_ref[...], ind).astype(BF16)


def _post_kernel(x1_ref, o_ref, yc_ref, p_ref, woa_ref, wob_ref, g2_ref, wg_ref, wu_ref, wd_ref,
                 gp_ref, wpg_ref, wpp_ref, out_ref):
    x2 = x1_ref[...] + _dot(o_ref[...], woa_ref[...]) + _dot(yc_ref[...], wob_ref[...])
    x3 = _swiglu_half_step(x2, g2_ref, wg_ref, wu_ref, wd_ref)
    gate = jax.nn.sigmoid(_dot(_rms(x3, gp_ref[...]).astype(BF16), wpg_ref[...]))
    out_ref[...] = x3 + gate * _dot(p_ref[...].astype(BF16), wpp_ref[...])


def _attn_kernel(qt_ref, k_ref, vt_ref, lq1_ref, lk1_ref, lq2_ref, lk2_ref, on_ref, o_ref,
                 qm, s_buf, mx_buf, p_buf, a_buf, m_sc, l_sc, acc, *, lam_init):
    tq, tk = Q_TILE, KV_TILE
    i = pl.program_id(1)
    qt = qt_ref[...]
    feat = lax.broadcasted_iota(jnp.int32, qt.shape, 0)
    zero = jnp.zeros_like(qt)
    qm[:, 0:tq] = jnp.where(feat < QK_DIM, qt, zero)
    qm[:, tq:2 * tq] = jnp.where(feat >= QK_DIM, qt, zero)
    m_sc[...] = jnp.full(m_sc.shape, -jnp.inf, F32)
    l_sc[...] = jnp.zeros(l_sc.shape, F32)
    acc[...] = jnp.zeros(acc.shape, F32)

    def visible_rows(lanes, diag):
        return (lanes.start % tq) + LANE_BLOCK if diag else tk

    def scores(tile, lanes, diag):
        rows = visible_rows(lanes, diag)
        start = pl.multiple_of(tile * tk, tk)
        st = _dot(k_ref[pl.ds(start, rows), :], qm[:, lanes])
        if diag:
            below, edge = st[:rows - LANE_BLOCK], st[rows - LANE_BLOCK:]
            kpos = lax.broadcasted_iota(jnp.int32, edge.shape, 0)
            qpos = lax.broadcasted_iota(jnp.int32, edge.shape, 1)
            edge = jnp.where(kpos <= qpos, edge, -jnp.inf)
            s_buf[rows - LANE_BLOCK:rows, lanes] = edge
            mx = jnp.max(edge, axis=0, keepdims=True)
            if rows > LANE_BLOCK:
                s_buf[0:rows - LANE_BLOCK, lanes] = below
                mx = jnp.maximum(mx, jnp.max(below, axis=0, keepdims=True))
            mx_buf[:, lanes] = mx
        else:
            s_buf[:, lanes] = st
            mx_buf[:, lanes] = jnp.max(st, axis=0, keepdims=True)

    def softmax(lanes, diag):
        rows = visible_rows(lanes, diag)
        m_prev = m_sc[:, lanes]
        m_new = jnp.maximum(m_prev, mx_buf[:, lanes])
        alpha = jnp.exp2(m_prev - m_new)
        p = jnp.exp2(s_buf[0:rows, lanes] - m_new)
        l_sc[:, lanes] = alpha * l_sc[:, lanes] + jnp.sum(p, axis=0, keepdims=True)
        p_buf[0:rows, lanes] = p.astype(BF16)
        if rows < tk:
            p_buf[rows:tk, lanes] = jnp.zeros((tk - rows, LANE_BLOCK), BF16)
        a_buf[:, lanes] = alpha
        m_sc[:, lanes] = m_new

    def value(tile, lanes):
        start = pl.multiple_of(tile * tk, tk)
        acc[:, lanes] = (acc[:, lanes] * a_buf[:, lanes]
                         + _dot(vt_ref[:, pl.ds(start, tk)], p_buf[:, lanes]))

    def step(value_tile=None, softmax_diag=None, scores_tile=None, scores_diag=False):
        for blk in range(2 * tq // LANE_BLOCK):
            lanes = slice(blk * LANE_BLOCK, (blk + 1) * LANE_BLOCK)
            if value_tile is not None:
                value(value_tile, lanes)
            if softmax_diag is not None:
                softmax(lanes, softmax_diag)
            if scores_tile is not None:
                scores(scores_tile, lanes, scores_diag)

    step(scores_tile=i, scores_diag=True)

    @pl.when(i == 0)
    def _():
        step(softmax_diag=True)
        step(value_tile=i)

    @pl.when(i > 0)
    def _():
        step(softmax_diag=True, scores_tile=0)

        def body(n, carry):
            step(value_tile=jnp.where(n == 2, i, n - 3), softmax_diag=False, scores_tile=n - 1)
            return carry

        lax.fori_loop(2, i + 1, body, 0)
        step(value_tile=jnp.where(i == 1, i, i - 2), softmax_diag=False)
        step(value_tile=i - 1)

    lam = (jnp.exp(jnp.sum(lq1_ref[...] * lk1_ref[...], axis=-1, keepdims=True))
           - jnp.exp(jnp.sum(lq2_ref[...] * lk2_ref[...], axis=-1, keepdims=True)) + lam_init)
    ot = acc[:, 0:tq] / l_sc[:, 0:tq] - lam * (acc[:, tq:2 * tq] / l_sc[:, tq:2 * tq])
    ot = ot * lax.rsqrt(jnp.mean(ot * ot, axis=0, keepdims=True) + EPS) * on_ref[...]
    o_ref[...] = (ot * (1.0 - lam_init)).T.astype(BF16)


def _resident(a):
    if isinstance(a, tuple):
        _, block_shape, block_index = a
        return pl.BlockSpec(block_shape, lambda *_: block_index, pipeline_mode=pl.Buffered(1))
    return pl.BlockSpec(a.shape, lambda *_: (0,) * a.ndim, pipeline_mode=pl.Buffered(1))


def _operand(a):
    return a[0] if isinstance(a, tuple) else a


def _layer(stacked, layer):
    return (stacked, (None,) + stacked.shape[1:], (layer, 0, 0))


def _rows(width, tm=ROW_TILE):
    return pl.BlockSpec((tm, width), lambda i: (i, 0))


def _pre_call(x, g1, wg, wu, wd, gm, win, qn, kn, cw, cb, cn, ind):
    s = x.shape[0]
    row_major = pl.BlockSpec((N_HEADS, ROW_TILE, HEAD_DIM), lambda i: (0, i, 0))
    feat_major = pl.BlockSpec((N_HEADS, HEAD_DIM, ROW_TILE), lambda i: (0, 0, i))
    consts = (g1, wg, wu, wd, gm, win, qn, kn, cw, cb, cn, ind)
    return pl.pallas_call(
        _pre_kernel,
        grid=(s // ROW_TILE,),
        in_specs=[_rows(D_MODEL)] + [_resident(c) for c in consts],
        out_specs=[_rows(D_MODEL), feat_major, row_major, feat_major, _rows(CONV_WIDTH)],
        out_shape=[jax.ShapeDtypeStruct((s, D_MODEL), F32),
                   jax.ShapeDtypeStruct((N_HEADS, HEAD_DIM, s), BF16),
                   jax.ShapeDtypeStruct((N_HEADS, s, HEAD_DIM), BF16),
                   jax.ShapeDtypeStruct((N_HEADS, HEAD_DIM, s), BF16),
                   jax.ShapeDtypeStruct((s, CONV_WIDTH), BF16)],
        scratch_shapes=[pltpu.VMEM((ROW_TILE + CARRY_ROWS, CONV_WIDTH), F32)],
        compiler_params=pltpu.CompilerParams(
            dimension_semantics=("arbitrary",), vmem_limit_bytes=VMEM_LIMIT_BYTES),
        name="pre_mix",
    )(x, *map(_operand, consts))


def _post_call(x1, o, yc, p, layer, woa, wob, g2, wg, wu, wd, gp, wpg, wpp):
    s = x1.shape[0]
    consts = (woa, wob, g2, wg, wu, wd, gp, wpg, wpp)
    p_rows = pl.BlockSpec((None, None, ROW_TILE, D_PLE), lambda i: (layer, 0, i, 0))
    return pl.pallas_call(
        _post_kernel,
        grid=(s // ROW_TILE,),
        in_specs=[_rows(D_MODEL), _rows(ATTN_WIDTH), _rows(CONV_WIDTH), p_rows]
        + [_resident(c) for c in consts],
        out_specs=_rows(D_MODEL),
        out_shape=jax.ShapeDtypeStruct((s, D_MODEL), F32),
        compiler_params=pltpu.CompilerParams(
            dimension_semantics=("parallel",), vmem_limit_bytes=VMEM_LIMIT_BYTES),
        name="post_mix",
    )(x1, o, yc, p, *map(_operand, consts))


def _attn_call(qt, k, vt, lq1, lk1, lq2, lk2, on, lam_init):
    s = k.shape[1]
    qt_spec = pl.BlockSpec((None, HEAD_DIM, Q_TILE), lambda h, i: (h, 0, i))
    single = pl.Buffered(1)
    k_spec = pl.BlockSpec((None, s, HEAD_DIM), lambda h, i: (h, 0, 0), pipeline_mode=single)
    vt_spec = pl.BlockSpec((None, HEAD_DIM, s), lambda h, i: (h, 0, 0), pipeline_mode=single)
    vec = pl.BlockSpec((1, QK_DIM), lambda h, i: (0, 0))
    return pl.pallas_call(
        functools.partial(_attn_kernel, lam_init=lam_init),
        grid=(N_HEADS, s // Q_TILE),
        in_specs=[qt_spec, k_spec, vt_spec, vec, vec, vec, vec,
                  pl.BlockSpec((HEAD_DIM, 1), lambda h, i: (0, 0))],
        out_specs=pl.BlockSpec((Q_TILE, HEAD_DIM), lambda h, i: (i, h)),
        out_shape=jax.ShapeDtypeStruct((s, ATTN_WIDTH), BF16),
        scratch_shapes=[pltpu.VMEM((HEAD_DIM, 2 * Q_TILE), BF16),
                        pltpu.VMEM((KV_TILE, 2 * Q_TILE), F32),
                        pltpu.VMEM((1, 2 * Q_TILE), F32),
                        pltpu.VMEM((KV_TILE, 2 * Q_TILE), BF16),
                        pltpu.VMEM((1, 2 * Q_TILE), F32),
                        pltpu.VMEM((1, 2 * Q_TILE), F32),
                        pltpu.VMEM((1, 2 * Q_TILE), F32),
                        pltpu.VMEM((HEAD_DIM, 2 * Q_TILE), F32)],
        compiler_params=pltpu.CompilerParams(
            dimension_semantics=("parallel", "parallel"), vmem_limit_bytes=VMEM_LIMIT_BYTES),
        name="diff_attn",
    )(qt, k, vt, lq1, lk1, lq2, lk2, on)


def kernel(x, p, ffn1_norm, ffn1_w_gate, ffn1_w_up, ffn1_w_down, mix_norm, w_in, q_norm, k_norm,
           lambda_q1, lambda_k1, lambda_q2, lambda_k2, attn_out_norm, conv_w, conv_b, conv_out_norm,
           w_out, ffn2_norm, ffn2_w_gate, ffn2_w_up, ffn2_w_down, ple_norm, ple_w_gate, ple_w_proj):
    b, s, _ = x.shape
    depth = p.shape[0]
    assert b == 1 and s % Q_TILE == 0 and s % ROW_TILE == 0 and Q_TILE == KV_TILE
    group = jnp.arange(MXU_TILE, dtype=jnp.int32) // GROUP_DIM
    ind = (group[:, None] == group[None, :]).astype(BF16)

    def row(a):
        return a.reshape(1, -1).astype(F32)

    def tiled(a):
        return jnp.tile(a.astype(F32), ATTN_WIDTH // a.shape[0]).reshape(1, ATTN_WIDTH)

    wg1, wu1, wd1, win, wo, wg2, wu2, wd2, wpg, wpp = (
        w.astype(BF16) for w in (ffn1_w_gate, ffn1_w_up, ffn1_w_down, w_in, w_out,
                                 ffn2_w_gate, ffn2_w_up, ffn2_w_down, ple_w_gate, ple_w_proj))
    xs = x.reshape(s, D_MODEL)
    for i in range(depth):
        lam_init = 0.8 - 0.6 * math.exp(-0.3 * i)
        x1, qt, k, vt, yc = _pre_call(
            xs, row(ffn1_norm[i]), _layer(wg1, i), _layer(wu1, i), _layer(wd1, i), row(mix_norm[i]),
            _layer(win, i), tiled(q_norm[i]), tiled(k_norm[i]), conv_w[i].astype(F32),
            row(conv_b[i]), row(conv_out_norm[i]), ind)
        o = _attn_call(qt, k, vt, row(lambda_q1[i]), row(lambda_k1[i]), row(lambda_q2[i]),
                       row(lambda_k2[i]), attn_out_norm[i].astype(F32).reshape(HEAD_DIM, 1), lam_init)
        wo_half = (None, ATTN_WIDTH, D_MODEL)
        xs = _post_call(
            x1, o, yc, p, i, (wo, wo_half, (i, 0, 0)), (wo, wo_half, (i, 1, 0)),
            row(ffn2_norm[i]), _layer(wg2, i), _layer(wu2, i), _layer(wd2, i), row(ple_norm[i]),
            _layer(wpg, i), _layer(wpp, i))
    return xs.reshape(b, s, D_MODEL)
```

```python
import functools
import math

import jax
import jax.numpy as jnp
from jax import lax
from jax.experimental import pallas as pl
from jax.experimental.pallas import tpu as pltpu

D_MODEL = 1024
D_FF = 2816
D_PLE = 256
ATTN_WIDTH = 512
CONV_WIDTH = 512
N_HEADS = 4
HEAD_DIM = 128
V_ROWS = HEAD_DIM + 16
QK_DIM = 64
GROUP_DIM = 64
CONV_K = 3
EPS = 1e-6

ROW_TILE = 512
Q_TILE = 1024
KV_TILE = 1024
MXU_TILE = 256
LANE_BLOCK = MXU_TILE
CARRY_ROWS = 8
VMEM_LIMIT_BYTES = 56 * 1024 * 1024

BF16 = jnp.bfloat16
F32 = jnp.float32


def _dot(a, b):
    return jnp.dot(a, b, preferred_element_type=F32)


def _rms(x, g):
    return x * lax.rsqrt(jnp.mean(x * x, axis=-1, keepdims=True) + EPS) * g


def _group_rms(x, g, ind):
    sq = x * x
    hi = sq.astype(BF16)
    lo = (sq - hi.astype(F32)).astype(BF16)
    sums = [_dot(hi[:, c:c + MXU_TILE], ind) + _dot(lo[:, c:c + MXU_TILE], ind)
            for c in range(0, x.shape[1], MXU_TILE)]
    ms = jnp.concatenate(sums, axis=1) * (1.0 / GROUP_DIM)
    return x * lax.rsqrt(ms + EPS) * g


def _swiglu_half_step(x, g_ref, wg_ref, wu_ref, wd_ref):
    n = _rms(x, g_ref[...]).astype(BF16)
    gate = _dot(n, wg_ref[...])
    up = _dot(n, wu_ref[...])
    h = (gate * jax.nn.sigmoid(gate) * up).astype(BF16)
    return x + 0.5 * _dot(h, wd_ref[...])


def _pre_kernel(x_ref, g1_ref, wg_ref, wu_ref, wd_ref, gm_ref, win_ref, qn_ref, kn_ref,
                cw_ref, cb_ref, cn_ref, ind_ref,
                x1_ref, q_ref, k_ref, v_ref, yc_ref, u_ext):
    tm = x_ref.shape[0]
    x1 = _swiglu_half_step(x_ref[...], g1_ref, wg_ref, wu_ref, wd_ref)
    x1_ref[...] = x1
    h = _rms(x1, gm_ref[...]).astype(BF16)
    ind = ind_ref[...]

    def z_cols(section):
        return _dot(h, win_ref[:, section * ATTN_WIDTH:(section + 1) * ATTN_WIDTH])

    q = _group_rms(z_cols(0), qn_ref[...], ind) * (math.log2(math.e) / math.sqrt(QK_DIM))
    k = _group_rms(z_cols(1), kn_ref[...], ind)
    v = z_cols(2)
    for hd in range(N_HEADS):
        cols = slice(hd * HEAD_DIM, (hd + 1) * HEAD_DIM)
        q_ref[hd] = q[:, cols].T.astype(BF16)
        k_ref[hd] = k[:, cols].astype(BF16)
        v_ref[hd, 0:HEAD_DIM, :] = v[:, cols].T.astype(BF16)
        v_ref[hd, HEAD_DIM:V_ROWS, :] = jnp.ones((V_ROWS - HEAD_DIM, tm), BF16)

    @pl.when(pl.program_id(0) == 0)
    def _():
        u_ext[0:CARRY_ROWS, :] = jnp.zeros((CARRY_ROWS, CONV_WIDTH), F32)

    u = z_cols(4) * z_cols(5)
    u_ext[CARRY_ROWS:CARRY_ROWS + tm, :] = u
    conv = (cw_ref[0:1, :] * u_ext[pl.ds(CARRY_ROWS - 2, tm), :]
            + cw_ref[1:2, :] * u_ext[pl.ds(CARRY_ROWS - 1, tm), :]
            + cw_ref[2:3, :] * u + cb_ref[...])
    u_ext[0:CARRY_ROWS, :] = u_ext[tm:tm + CARRY_ROWS, :]
    yc = z_cols(3) * conv
    yc_ref[...] = _group_rms(yc, cn_ref[...], ind).astype(BF16)


def _post_kernel(x1_ref, o_ref, yc_ref, p_ref, woa_ref, wob_ref, g2_ref, wg_ref, wu_ref, wd_ref,
                 gp_ref, wpg_ref, wpp_ref, out_ref):
    x2 = x1_ref[...] + _dot(o_ref[...], woa_ref[...]) + _dot(yc_ref[...], wob_ref[...])
    x3 = _swiglu_half_step(x2, g2_ref, wg_ref, wu_ref, wd_ref)
    gate = jax.nn.sigmoid(_dot(_rms(x3, gp_ref[...]).astype(BF16), wpg_ref[...]))
    out_ref[...] = x3 + gate * _dot(p_ref[...].astype(BF16), wpp_ref[...])


def _attn_kernel(qt_ref, k_ref, vt_ref, lq1_ref, lk1_ref, lq2_ref, lk2_ref, on_ref, o_ref,
                 qm, s_buf, mx_buf, p_buf, a_buf, m_sc, acc, *, lam_init):
    tq, tk = Q_TILE, KV_TILE
    i = pl.program_id(1)
    qt = qt_ref[...]
    feat = lax.broadcasted_iota(jnp.int32, qt.shape, 0)
    zero = jnp.zeros_like(qt)
    qm[:, 0:tq] = jnp.where(feat < QK_DIM, qt, zero)
    qm[:, tq:2 * tq] = jnp.where(feat >= QK_DIM, qt, zero)
    m_sc[...] = jnp.full(m_sc.shape, -jnp.inf, F32)
    acc[...] = jnp.zeros(acc.shape, F32)

    def visible_rows(lanes, diag):
        return (lanes.start % tq) + LANE_BLOCK if diag else tk

    def scores(tile, lanes, diag):
        rows = visible_rows(lanes, diag)
        start = pl.multiple_of(tile * tk, tk)
        st = _dot(k_ref[pl.ds(start, rows), :], qm[:, lanes])
        if diag:
            below, edge = st[:rows - LANE_BLOCK], st[rows - LANE_BLOCK:]
            kpos = lax.broadcasted_iota(jnp.int32, edge.shape, 0)
            qpos = lax.broadcasted_iota(jnp.int32, edge.shape, 1)
            edge = jnp.where(kpos <= qpos, edge, -jnp.inf)
            s_buf[rows - LANE_BLOCK:rows, lanes] = edge
            mx = jnp.max(edge, axis=0, keepdims=True)
            if rows > LANE_BLOCK:
                s_buf[0:rows - LANE_BLOCK, lanes] = below
                mx = jnp.maximum(mx, jnp.max(below, axis=0, keepdims=True))
            mx_buf[:, lanes] = mx
        else:
            s_buf[:, lanes] = st
            mx_buf[:, lanes] = jnp.max(st, axis=0, keepdims=True)

    def softmax(lanes, diag):
        rows = visible_rows(lanes, diag)
        m_prev = m_sc[:, lanes]
        m_new = jnp.maximum(m_prev, mx_buf[:, lanes])
        alpha = jnp.exp2(m_prev - m_new)
        p = jnp.exp2(s_buf[0:rows, lanes] - m_new)
        p_buf[0:rows, lanes] = p.astype(BF16)
        if rows < tk:
            p_buf[rows:tk, lanes] = jnp.zeros((tk - rows, LANE_BLOCK), BF16)
        a_buf[:, lanes] = alpha
        m_sc[:, lanes] = m_new

    def value(tile, lanes):
        start = pl.multiple_of(tile * tk, tk)
        acc[:, lanes] = (acc[:, lanes] * a_buf[:, lanes]
                         + _dot(vt_ref[:, pl.ds(start, tk)], p_buf[:, lanes]))

    def step(value_tile=None, softmax_diag=None, scores_tile=None, scores_diag=False):
        for blk in range(2 * tq // LANE_BLOCK):
            lanes = slice(blk * LANE_BLOCK, (blk + 1) * LANE_BLOCK)
            if value_tile is not None:
                value(value_tile, lanes)
            if softmax_diag is not None:
                softmax(lanes, softmax_diag)
            if scores_tile is not None:
                scores(scores_tile, lanes, scores_diag)

    step(scores_tile=i, scores_diag=True)

    @pl.when(i == 0)
    def _():
        step(softmax_diag=True)
        step(value_tile=i)

    @pl.when(i > 0)
    def _():
        step(softmax_diag=True, scores_tile=0)

        def body(n, carry):
            step(value_tile=jnp.where(n == 2, i, n - 3), softmax_diag=False, scores_tile=n - 1)
            return carry

        lax.fori_loop(2, i + 1, body, 0)
        step(value_tile=jnp.where(i == 1, i, i - 2), softmax_diag=False)
        step(value_tile=i - 1)

    lam = (jnp.exp(jnp.sum(lq1_ref[...] * lk1_ref[...], axis=-1, keepdims=True))
           - jnp.exp(jnp.sum(lq2_ref[...] * lk2_ref[...], axis=-1, keepdims=True)) + lam_init)
    num, den = acc[0:HEAD_DIM, :], acc[HEAD_DIM:HEAD_DIM + 1, :]
    ot = num[:, 0:tq] / den[:, 0:tq] - lam * (num[:, tq:2 * tq] / den[:, tq:2 * tq])
    ot = ot * lax.rsqrt(jnp.mean(ot * ot, axis=0, keepdims=True) + EPS) * on_ref[...]
    o_ref[...] = (ot * (1.0 - lam_init)).T.astype(BF16)


def _resident(a):
    if isinstance(a, tuple):
        _, block_shape, block_index = a
        return pl.BlockSpec(block_shape, lambda *_: block_index, pipeline_mode=pl.Buffered(1))
    return pl.BlockSpec(a.shape, lambda *_: (0,) * a.ndim, pipeline_mode=pl.Buffered(1))


def _operand(a):
    return a[0] if isinstance(a, tuple) else a


def _layer(stacked, layer):
    return (stacked, (None,) + stacked.shape[1:], (layer, 0, 0))


def _rows(width, tm=ROW_TILE):
    return pl.BlockSpec((tm, width), lambda i: (i, 0))


def _pre_call(x, g1, wg, wu, wd, gm, win, qn, kn, cw, cb, cn, ind):
    s = x.shape[0]
    row_major = pl.BlockSpec((N_HEADS, ROW_TILE, HEAD_DIM), lambda i: (0, i, 0))
    feat_major = pl.BlockSpec((N_HEADS, HEAD_DIM, ROW_TILE), lambda i: (0, 0, i))
    v_major = pl.BlockSpec((N_HEADS, V_ROWS, ROW_TILE), lambda i: (0, 0, i))
    consts = (g1, wg, wu, wd, gm, win, qn, kn, cw, cb, cn, ind)
    return pl.pallas_call(
        _pre_kernel,
        grid=(s // ROW_TILE,),
        in_specs=[_rows(D_MODEL)] + [_resident(c) for c in consts],
        out_specs=[_rows(D_MODEL), feat_major, row_major, v_major, _rows(CONV_WIDTH)],
        out_shape=[jax.ShapeDtypeStruct((s, D_MODEL), F32),
                   jax.ShapeDtypeStruct((N_HEADS, HEAD_DIM, s), BF16),
                   jax.ShapeDtypeStruct((N_HEADS, s, HEAD_DIM), BF16),
                   jax.ShapeDtypeStruct((N_HEADS, V_ROWS, s), BF16),
                   jax.ShapeDtypeStruct((s, CONV_WIDTH), BF16)],
        scratch_shapes=[pltpu.VMEM((ROW_TILE + CARRY_ROWS, CONV_WIDTH), F32)],
        compiler_params=pltpu.CompilerParams(
            dimension_semantics=("arbitrary",), vmem_limit_bytes=VMEM_LIMIT_BYTES),
        name="pre_mix",
    )(x, *map(_operand, consts))


def _post_call(x1, o, yc, p, layer, woa, wob, g2, wg, wu, wd, gp, wpg, wpp):
    s = x1.shape[0]
    consts = (woa, wob, g2, wg, wu, wd, gp, wpg, wpp)
    p_rows = pl.BlockSpec((None, None, ROW_TILE, D_PLE), lambda i: (layer, 0, i, 0))
    return pl.pallas_call(
        _post_kernel,
        grid=(s // ROW_TILE,),
        in_specs=[_rows(D_MODEL), _rows(ATTN_WIDTH), _rows(CONV_WIDTH), p_rows]
        + [_resident(c) for c in consts],
        out_specs=_rows(D_MODEL),
        out_shape=jax.ShapeDtypeStruct((s, D_MODEL), F32),
        compiler_params=pltpu.CompilerParams(
            dimension_semantics=("parallel",), vmem_limit_bytes=VMEM_LIMIT_BYTES),
        name="post_mix",
    )(x1, o, yc, p, *map(_operand, consts))


def _attn_call(qt, k, vt, lq1, lk1, lq2, lk2, on, lam_init):
    s = k.shape[1]
    qt_spec = pl.BlockSpec((None, HEAD_DIM, Q_TILE), lambda h, i: (h, 0, i))
    single = pl.Buffered(1)
    k_spec = pl.BlockSpec((None, s, HEAD_DIM), lambda h, i: (h, 0, 0), pipeline_mode=single)
    vt_spec = pl.BlockSpec((None, V_ROWS, s), lambda h, i: (h, 0, 0), pipeline_mode=single)
    vec = pl.BlockSpec((1, QK_DIM), lambda h, i: (0, 0))
    return pl.pallas_call(
        functools.partial(_attn_kernel, lam_init=lam_init),
        grid=(N_HEADS, s // Q_TILE),
        in_specs=[qt_spec, k_spec, vt_spec, vec, vec, vec, vec,
                  pl.BlockSpec((HEAD_DIM, 1), lambda h, i: (0, 0))],
        out_specs=pl.BlockSpec((Q_TILE, HEAD_DIM), lambda h, i: (i, h)),
        out_shape=jax.ShapeDtypeStruct((s, ATTN_WIDTH), BF16),
        scratch_shapes=[pltpu.VMEM((HEAD_DIM, 2 * Q_TILE), BF16),
                        pltpu.VMEM((KV_TILE, 2 * Q_TILE), F32),
                        pltpu.VMEM((1, 2 * Q_TILE), F32),
                        pltpu.VMEM((KV_TILE, 2 * Q_TILE), BF16),
                        pltpu.VMEM((1, 2 * Q_TILE), F32),
                        pltpu.VMEM((1, 2 * Q_TILE), F32),
                        pltpu.VMEM((V_ROWS, 2 * Q_TILE), F32)],
        compiler_params=pltpu.CompilerParams(
            dimension_semantics=("parallel", "parallel"), vmem_limit_bytes=VMEM_LIMIT_BYTES),
        name="diff_attn",
    )(qt, k, vt, lq1, lk1, lq2, lk2, on)


def kernel(x, p, ffn1_norm, ffn1_w_gate, ffn1_w_up, ffn1_w_down, mix_norm, w_in, q_norm, k_norm,
           lambda_q1, lambda_k1, lambda_q2, lambda_k2, attn_out_norm, conv_w, conv_b, conv_out_norm,
           w_out, ffn2_norm, ffn2_w_gate, ffn2_w_up, ffn2_w_down, ple_norm, ple_w_gate, ple_w_proj):
    b, s, _ = x.shape
    depth = p.shape[0]
    assert b == 1 and s % Q_TILE == 0 and s % ROW_TILE == 0 and Q_TILE == KV_TILE
    group = jnp.arange(MXU_TILE, dtype=jnp.int32) // GROUP_DIM
    ind = (group[:, None] == group[None, :]).astype(BF16)

    def row(a):
        return a.reshape(1, -1).astype(F32)

    def tiled(a):
        return jnp.tile(a.astype(F32), ATTN_WIDTH // a.shape[0]).reshape(1, ATTN_WIDTH)

    wg1, wu1, wd1, win, wo, wg2, wu2, wd2, wpg, wpp = (
        w.astype(BF16) for w in (ffn1_w_gate, ffn1_w_up, ffn1_w_down, w_in, w_out,
                                 ffn2_w_gate, ffn2_w_up, ffn2_w_down, ple_w_gate, ple_w_proj))
    xs = x.reshape(s, D_MODEL)
    for i in range(depth):
        lam_init = 0.8 - 0.6 * math.exp(-0.3 * i)
        x1, qt, k, vt, yc = _pre_call(
            xs, row(ffn1_norm[i]), _layer(wg1, i), _layer(wu1, i), _layer(wd1, i), row(mix_norm[i]),
            _layer(win, i), tiled(q_norm[i]), tiled(k_norm[i]), conv_w[i].astype(F32),
            row(conv_b[i]), row(conv_out_norm[i]), ind)
        o = _attn_call(qt, k, vt, row(lambda_q1[i]), row(lambda_k1[i]), row(lambda_q2[i]),
                       row(lambda_k2[i]), attn_out_norm[i].astype(F32).reshape(HEAD_DIM, 1), lam_init)
        wo_half = (None, ATTN_WIDTH, D_MODEL)
        xs = _post_call(
            x1, o, yc, p, i, (wo, wo_half, (i, 0, 0)), (wo, wo_half, (i, 1, 0)),
            row(ffn2_norm[i]), _layer(wg2, i), _layer(wu2, i), _layer(wd2, i), row(ple_norm[i]),
            _layer(wpg, i), _layer(wpp, i))
    return xs.reshape(b, s, D_MODEL)
```

```python
import functools
import math

import jax
import jax.numpy as jnp
from jax import lax
from jax.experimental import pallas as pl
from jax.experimental.pallas import tpu as pltpu

D_MODEL = 1024
D_FF = 2816
D_PLE = 256
ATTN_WIDTH = 512
CONV_WIDTH = 512
N_HEADS = 4
HEAD_DIM = 128
V_ROWS = HEAD_DIM + 16
QK_DIM = 64
GROUP_DIM = 64
CONV_K = 3
EPS = 1e-6

ROW_TILE = 512
Q_TILE = 1024
KV_TILE = 1024
MXU_TILE = 256
LANE_BLOCK = MXU_TILE
CARRY_ROWS = 8
VMEM_LIMIT_BYTES = 56 * 1024 * 1024

BF16 = jnp.bfloat16
F32 = jnp.float32


def _dot(a, b):
    return jnp.dot(a, b, preferred_element_type=F32)


def _rms(x, g):
    return x * lax.rsqrt(jnp.mean(x * x, axis=-1, keepdims=True) + EPS) * g


def _group_rms(x, g, ind):
    sq = x * x
    hi = sq.astype(BF16)
    lo = (sq - hi.astype(F32)).astype(BF16)
    sums = [_dot(hi[:, c:c + MXU_TILE], ind) + _dot(lo[:, c:c + MXU_TILE], ind)
            for c in range(0, x.shape[1], MXU_TILE)]
    ms = jnp.concatenate(sums, axis=1) * (1.0 / GROUP_DIM)
    return x * lax.rsqrt(ms + EPS) * g


def _swiglu_half_step(x, g_ref, wg_ref, wu_ref, wd_ref):
    n = _rms(x, g_ref[...]).astype(BF16)
    gate = _dot(n, wg_ref[...])
    up = _dot(n, wu_ref[...])
    h = (gate * jax.nn.sigmoid(gate) * up).astype(BF16)
    return x + 0.5 * _dot(h, wd_ref[...])


def _pre_kernel(x_ref, g1_ref, wg_ref, wu_ref, wd_ref, gm_ref, win_ref, qn_ref, kn_ref,
                cw_ref, cb_ref, cn_ref, ind_ref,
                x1_ref, q_ref, k_ref, v_ref, yc_ref, u_ext):
    tm = x_ref.shape[0]
    x1 = _swiglu_half_step(x_ref[...], g1_ref, wg_ref, wu_ref, wd_ref)
    x1_ref[...] = x1
    h = _rms(x1, gm_ref[...]).astype(BF16)
    ind = ind_ref[...]

    def z_cols(section):
        return _dot(h, win_ref[:, section * ATTN_WIDTH:(section + 1) * ATTN_WIDTH])

    q = _group_rms(z_cols(0), qn_ref[...], ind) * (math.log2(math.e) / math.sqrt(QK_DIM))
    k = _group_rms(z_cols(1), kn_ref[...], ind)
    v = z_cols(2)
    for hd in range(N_HEADS):
        cols = slice(hd * HEAD_DIM, (hd + 1) * HEAD_DIM)
        q_ref[hd] = q[:, cols].T.astype(BF16)
        k_ref[hd] = k[:, cols].astype(BF16)
        v_ref[hd, 0:HEAD_DIM, :] = v[:, cols].T.astype(BF16)
        v_ref[hd, HEAD_DIM:V_ROWS, :] = jnp.ones((V_ROWS - HEAD_DIM, tm), BF16)

    @pl.when(pl.program_id(0) == 0)
    def _():
        u_ext[0:CARRY_ROWS, :] = jnp.zeros((CARRY_ROWS, CONV_WIDTH), F32)

    u = z_cols(4) * z_cols(5)
    u_ext[CARRY_ROWS:CARRY_ROWS + tm, :] = u
    conv = (cw_ref[0:1, :] * u_ext[pl.ds(CARRY_ROWS - 2, tm), :]
            + cw_ref[1:2, :] * u_ext[pl.ds(CARRY_ROWS - 1, tm), :]
            + cw_ref[2:3, :] * u + cb_ref[...])
    u_ext[0:CARRY_ROWS, :] = u_ext[tm:tm + CARRY_ROWS, :]
    yc = z_cols(3) * conv
    yc_ref[...] = _group_rms(yc, cn_ref[...], ind).astype(BF16)


def _post_kernel(x1_ref, o_ref, yc_ref, p_ref, woa_ref, wob_ref, g2_ref, wg_ref, wu_ref, wd_ref,
                 gp_ref, wpg_ref, wpp_ref, out_ref):
    x2 = x1_ref[...] + _dot(o_ref[...], woa_ref[...]) + _dot(yc_ref[...], wob_ref[...])
    x3 = _swiglu_half_step(x2, g2_ref, wg_ref, wu_ref, wd_ref)
    gate = jax.nn.sigmoid(_dot(_rms(x3, gp_ref[...]).astype(BF16), wpg_ref[...]))
    out_ref[...] = x3 + gate * _dot(p_ref[...].astype(BF16), wpp_ref[...])


def _attn_kernel(qt_ref, k_ref, vt_ref, lq1_ref, lk1_ref, lq2_ref, lk2_ref, on_ref, o_ref,
                 qm, s_buf, mx_buf, p_buf, a_buf, m_sc, acc, *, lam_init):
    tq, tk = Q_TILE, KV_TILE
    i = pl.program_id(1)
    qt = qt_ref[...]
    feat = lax.broadcasted_iota(jnp.int32, qt.shape, 0)
    zero = jnp.zeros_like(qt)
    qm[:, 0:tq] = jnp.where(feat < QK_DIM, qt, zero)
    qm[:, tq:2 * tq] = jnp.where(feat >= QK_DIM, qt, zero)
    m_sc[...] = jnp.full(m_sc.shape, -jnp.inf, F32)
    acc[...] = jnp.zeros(acc.shape, F32)

    def visible_rows(lanes, diag):
        return (lanes.start % tq) + LANE_BLOCK if diag else tk

    def scores(tile, lanes, diag):
        rows = visible_rows(lanes, diag)
        start = pl.multiple_of(tile * tk, tk)
        st = _dot(k_ref[pl.ds(start, rows), :], qm[:, lanes])
        if diag:
            below, edge = st[:rows - LANE_BLOCK], st[rows - LANE_BLOCK:]
            kpos = lax.broadcasted_iota(jnp.int32, edge.shape, 0)
            qpos = lax.broadcasted_iota(jnp.int32, edge.shape, 1)
            edge = jnp.where(kpos <= qpos, edge, -jnp.inf)
            s_buf[rows - LANE_BLOCK:rows, lanes] = edge
            mx = jnp.max(edge, axis=0, keepdims=True)
            if rows > LANE_BLOCK:
                s_buf[0:rows - LANE_BLOCK, lanes] = below
                mx = jnp.maximum(mx, jnp.max(below, axis=0, keepdims=True))
            mx_buf[:, lanes] = mx
        else:
            s_buf[:, lanes] = st
            mx_buf[:, lanes] = jnp.max(st, axis=0, keepdims=True)

    def softmax(lanes, diag):
        rows = visible_rows(lanes, diag)
        m_prev = m_sc[:, lanes]
        m_new = jnp.maximum(m_prev, mx_buf[:, lanes])
        alpha = jnp.exp2(m_prev - m_new)
        p_buf[0:rows, lanes] = jnp.exp2((s_buf[0:rows, lanes] - m_new).astype(BF16))
        if rows < tk:
            p_buf[rows:tk, lanes] = jnp.zeros((tk - rows, LANE_BLOCK), BF16)
        a_buf[:, lanes] = alpha
        m_sc[:, lanes] = m_new

    def value(tile, lanes):
        start = pl.multiple_of(tile * tk, tk)
        acc[:, lanes] = (acc[:, lanes] * a_buf[:, lanes]
                         + _dot(vt_ref[:, pl.ds(start, tk)], p_buf[:, lanes]))

    def step(value_tile=None, softmax_diag=None, scores_tile=None, scores_diag=False):
        for blk in range(2 * tq // LANE_BLOCK):
            lanes = slice(blk * LANE_BLOCK, (blk + 1) * LANE_BLOCK)
            if value_tile is not None:
                value(value_tile, lanes)
            if softmax_diag is not None:
                softmax(lanes, softmax_diag)
            if scores_tile is not None:
                scores(scores_tile, lanes, scores_diag)

    step(scores_tile=i, scores_diag=True)

    @pl.when(i == 0)
    def _():
        step(softmax_diag=True)
        step(value_tile=i)

    @pl.when(i > 0)
    def _():
        step(softmax_diag=True, scores_tile=0)

        def body(n, carry):
            step(value_tile=jnp.where(n == 2, i, n - 3), softmax_diag=False, scores_tile=n - 1)
            return carry

        lax.fori_loop(2, i + 1, body, 0)
        step(value_tile=jnp.where(i == 1, i, i - 2), softmax_diag=False)
        step(value_tile=i - 1)

    lam = (jnp.exp(jnp.sum(lq1_ref[...] * lk1_ref[...], axis=-1, keepdims=True))
           - jnp.exp(jnp.sum(lq2_ref[...] * lk2_ref[...], axis=-1, keepdims=True)) + lam_init)
    num, den = acc[0:HEAD_DIM, :], acc[HEAD_DIM:HEAD_DIM + 1, :]
    ot = num[:, 0:tq] / den[:, 0:tq] - lam * (num[:, tq:2 * tq] / den[:, tq:2 * tq])
    ot = ot * lax.rsqrt(jnp.mean(ot * ot, axis=0, keepdims=True) + EPS) * on_ref[...]
    o_ref[...] = (ot * (1.0 - lam_init)).T.astype(BF16)


def _resident(a):
    if isinstance(a, tuple):
        _, block_shape, block_index = a
        return pl.BlockSpec(block_shape, lambda *_: block_index, pipeline_mode=pl.Buffered(1))
    return pl.BlockSpec(a.shape, lambda *_: (0,) * a.ndim, pipeline_mode=pl.Buffered(1))


def _operand(a):
    return a[0] if isinstance(a, tuple) else a


def _layer(stacked, layer):
    return (stacked, (None,) + stacked.shape[1:], (layer, 0, 0))


def _rows(width, tm=ROW_TILE):
    return pl.BlockSpec((tm, width), lambda i: (i, 0))


def _pre_call(x, g1, wg, wu, wd, gm, win, qn, kn, cw, cb, cn, ind):
    s = x.shape[0]
    row_major = pl.BlockSpec((N_HEADS, ROW_TILE, HEAD_DIM), lambda i: (0, i, 0))
    feat_major = pl.BlockSpec((N_HEADS, HEAD_DIM, ROW_TILE), lambda i: (0, 0, i))
    v_major = pl.BlockSpec((N_HEADS, V_ROWS, ROW_TILE), lambda i: (0, 0, i))
    consts = (g1, wg, wu, wd, gm, win, qn, kn, cw, cb, cn, ind)
    return pl.pallas_call(
        _pre_kernel,
        grid=(s // ROW_TILE,),
        in_specs=[_rows(D_MODEL)] + [_resident(c) for c in consts],
        out_specs=[_rows(D_MODEL), feat_major, row_major, v_major, _rows(CONV_WIDTH)],
        out_shape=[jax.ShapeDtypeStruct((s, D_MODEL), F32),
                   jax.ShapeDtypeStruct((N_HEADS, HEAD_DIM, s), BF16),
                   jax.ShapeDtypeStruct((N_HEADS, s, HEAD_DIM), BF16),
                   jax.ShapeDtypeStruct((N_HEADS, V_ROWS, s), BF16),
                   jax.ShapeDtypeStruct((s, CONV_WIDTH), BF16)],
        scratch_shapes=[pltpu.VMEM((ROW_TILE + CARRY_ROWS, CONV_WIDTH), F32)],
        compiler_params=pltpu.CompilerParams(
            dimension_semantics=("arbitrary",), vmem_limit_bytes=VMEM_LIMIT_BYTES),
        name="pre_mix",
    )(x, *map(_operand, consts))


def _post_call(x1, o, yc, p, layer, woa, wob, g2, wg, wu, wd, gp, wpg, wpp):
    s = x1.shape[0]
    consts = (woa, wob, g2, wg, wu, wd, gp, wpg, wpp)
    p_rows = pl.BlockSpec((None, None, ROW_TILE, D_PLE), lambda i: (layer, 0, i, 0))
    return pl.pallas_call(
        _post_kernel,
        grid=(s // ROW_TILE,),
        in_specs=[_rows(D_MODEL), _rows(ATTN_WIDTH), _rows(CONV_WIDTH), p_rows]
        + [_resident(c) for c in consts],
        out_specs=_rows(D_MODEL),
        out_shape=jax.ShapeDtypeStruct((s, D_MODEL), F32),
        compiler_params=pltpu.CompilerParams(
            dimension_semantics=("parallel",), vmem_limit_bytes=VMEM_LIMIT_BYTES),
        name="post_mix",
    )(x1, o, yc, p, *map(_operand, consts))


def _attn_call(qt, k, vt, lq1, lk1, lq2, lk2, on, lam_init):
    s = k.shape[1]
    qt_spec = pl.BlockSpec((None, HEAD_DIM, Q_TILE), lambda h, i: (h, 0, i))
    single = pl.Buffered(1)
    k_spec = pl.BlockSpec((None, s, HEAD_DIM), lambda h, i: (h, 0, 0), pipeline_mode=single)
    vt_spec = pl.BlockSpec((None, V_ROWS, s), lambda h, i: (h, 0, 0), pipeline_mode=single)
    vec = pl.BlockSpec((1, QK_DIM), lambda h, i: (0, 0))
    return pl.pallas_call(
        functools.partial(_attn_kernel, lam_init=lam_init),
        grid=(N_HEADS, s // Q_TILE),
        in_specs=[qt_spec, k_spec, vt_spec, vec, vec, vec, vec,
                  pl.BlockSpec((HEAD_DIM, 1), lambda h, i: (0, 0))],
        out_specs=pl.BlockSpec((Q_TILE, HEAD_DIM), lambda h, i: (i, h)),
        out_shape=jax.ShapeDtypeStruct((s, ATTN_WIDTH), BF16),
        scratch_shapes=[pltpu.VMEM((HEAD_DIM, 2 * Q_TILE), BF16),
                        pltpu.VMEM((KV_TILE, 2 * Q_TILE), F32),
                        pltpu.VMEM((1, 2 * Q_TILE), F32),
                        pltpu.VMEM((KV_TILE, 2 * Q_TILE), BF16),
                        pltpu.VMEM((1, 2 * Q_TILE), F32),
                        pltpu.VMEM((1, 2 * Q_TILE), F32),
                        pltpu.VMEM((V_ROWS, 2 * Q_TILE), F32)],
        compiler_params=pltpu.CompilerParams(
            dimension_semantics=("parallel", "parallel"), vmem_limit_bytes=VMEM_LIMIT_BYTES),
        name="diff_attn",
    )(qt, k, vt, lq1, lk1, lq2, lk2, on)


def kernel(x, p, ffn1_norm, ffn1_w_gate, ffn1_w_up, ffn1_w_down, mix_norm, w_in, q_norm, k_norm,
           lambda_q1, lambda_k1, lambda_q2, lambda_k2, attn_out_norm, conv_w, conv_b, conv_out_norm,
           w_out, ffn2_norm, ffn2_w_gate, ffn2_w_up, ffn2_w_down, ple_norm, ple_w_gate, ple_w_proj):
    b, s, _ = x.shape
    depth = p.shape[0]
    assert b == 1 and s % Q_TILE == 0 and s % ROW_TILE == 0 and Q_TILE == KV_TILE
    group = jnp.arange(MXU_TILE, dtype=jnp.int32) // GROUP_DIM
    ind = (group[:, None] == group[None, :]).astype(BF16)

    def row(a):
        return a.reshape(1, -1).astype(F32)

    def tiled(a):
        return jnp.tile(a.astype(F32), ATTN_WIDTH // a.shape[0]).reshape(1, ATTN_WIDTH)

    wg1, wu1, wd1, win, wo, wg2, wu2, wd2, wpg, wpp = (
        w.astype(BF16) for w in (ffn1_w_gate, ffn1_w_up, ffn1_w_down, w_in, w_out,
                                 ffn2_w_gate, ffn2_w_up, ffn2_w_down, ple_w_gate, ple_w_proj))
    xs = x.reshape(s, D_MODEL)
    for i in range(depth):
        lam_init = 0.8 - 0.6 * math.exp(-0.3 * i)
        x1, qt, k, vt, yc = _pre_call(
            xs, row(ffn1_norm[i]), _layer(wg1, i), _layer(wu1, i), _layer(wd1, i), row(mix_norm[i]),
            _layer(win, i), tiled(q_norm[i]), tiled(k_norm[i]), conv_w[i].astype(F32),
            row(conv_b[i]), row(conv_out_norm[i]), ind)
        o = _attn_call(qt, k, vt, row(lambda_q1[i]), row(lambda_k1[i]), row(lambda_q2[i]),
                       row(lambda_k2[i]), attn_out_norm[i].astype(F32).reshape(HEAD_DIM, 1), lam_init)
        wo_half = (None, ATTN_WIDTH, D_MODEL)
        xs = _post_call(
            x1, o, yc, p, i, (wo, wo_half, (i, 0, 0)), (wo, wo_half, (i, 1, 0)),
            row(ffn2_norm[i]), _layer(wg2, i), _layer(wu2, i), _layer(wd2, i), row(ple_norm[i]),
            _layer(wpg, i), _layer(wpp, i))
    return xs.reshape(b, s, D_MODEL)
```

```python
import functools
import math

import jax
import jax.numpy as jnp
from jax import lax
from jax.experimental import pallas as pl
from jax.experimental.pallas import tpu as pltpu

D_MODEL = 1024
D_FF = 2816
D_PLE = 256
ATTN_WIDTH = 512
CONV_WIDTH = 512
N_HEADS = 4
HEAD_DIM = 128
V_ROWS = HEAD_DIM + 16
QK_DIM = 64
GROUP_DIM = 64
CONV_K = 3
EPS = 1e-6

ROW_TILE = 512
Q_TILE = 1024
KV_TILE = 1024
MXU_TILE = 256
LANE_BLOCK = MXU_TILE
CARRY_ROWS = 8
VMEM_LIMIT_BYTES = 56 * 1024 * 1024

BF16 = jnp.bfloat16
F32 = jnp.float32


def _dot(a, b):
    return jnp.dot(a, b, preferred_element_type=F32)


def _rms(x, g):
    return x * lax.rsqrt(jnp.mean(x * x, axis=-1, keepdims=True) + EPS) * g


def _group_rms(x, g, ind):
    sq = x * x
    hi = sq.astype(BF16)
    lo = (sq - hi.astype(F32)).astype(BF16)
    sums = [_dot(hi[:, c:c + MXU_TILE], ind) + _dot(lo[:, c:c + MXU_TILE], ind)
            for c in range(0, x.shape[1], MXU_TILE)]
    ms = jnp.concatenate(sums, axis=1) * (1.0 / GROUP_DIM)
    return x * lax.rsqrt(ms + EPS) * g


def _swiglu_half_step(x, g_ref, wg_ref, wu_ref, wd_ref):
    n = _rms(x, g_ref[...]).astype(BF16)
    gate = _dot(n, wg_ref[...])
    up = _dot(n, wu_ref[...])
    h = (gate * jax.nn.sigmoid(gate) * up).astype(BF16)
    return x + 0.5 * _dot(h, wd_ref[...])


def _pre_kernel(x_ref, g1_ref, wg_ref, wu_ref, wd_ref, gm_ref, win_ref, qn_ref, kn_ref,
                cw_ref, cb_ref, cn_ref, ind_ref,
                x1_ref, q_ref, k_ref, v_ref, yc_ref, u_ext):
    tm = x_ref.shape[0]
    x1 = _swiglu_half_step(x_ref[...], g1_ref, wg_ref, wu_ref, wd_ref)
    x1_ref[...] = x1
    h = _rms(x1, gm_ref[...]).astype(BF16)
    ind = ind_ref[...]

    def z_cols(section):
        return _dot(h, win_ref[:, section * ATTN_WIDTH:(section + 1) * ATTN_WIDTH])

    q = _group_rms(z_cols(0), qn_ref[...], ind) * (math.log2(math.e) / math.sqrt(QK_DIM))
    k = _group_rms(z_cols(1), kn_ref[...], ind)
    v = z_cols(2)
    for hd in range(N_HEADS):
        cols = slice(hd * HEAD_DIM, (hd + 1) * HEAD_DIM)
        q_ref[hd] = q[:, cols].T.astype(BF16)
        k_ref[hd] = k[:, cols].astype(BF16)
        v_ref[hd, 0:HEAD_DIM, :] = v[:, cols].T.astype(BF16)
        v_ref[hd, HEAD_DIM:V_ROWS, :] = jnp.ones((V_ROWS - HEAD_DIM, tm), BF16)

    @pl.when(pl.program_id(0) == 0)
    def _():
        u_ext[0:CARRY_ROWS, :] = jnp.zeros((CARRY_ROWS, CONV_WIDTH), F32)

    u = z_cols(4) * z_cols(5)
    u_ext[CARRY_ROWS:CARRY_ROWS + tm, :] = u
    conv = (cw_ref[0:1, :] * u_ext[pl.ds(CARRY_ROWS - 2, tm), :]
            + cw_ref[1:2, :] * u_ext[pl.ds(CARRY_ROWS - 1, tm), :]
            + cw_ref[2:3, :] * u + cb_ref[...])
    u_ext[0:CARRY_ROWS, :] = u_ext[tm:tm + CARRY_ROWS, :]
    yc = z_cols(3) * conv
    yc_ref[...] = _group_rms(yc, cn_ref[...], ind).astype(BF16)


def _post_kernel(x1_ref, o_ref, yc_ref, p_ref, woa_ref, wob_ref, g2_ref, wg_ref, wu_ref, wd_ref,
                 gp_ref, wpg_ref, wpp_ref, out_ref):
    x2 = x1_ref[...] + _dot(o_ref[...], woa_ref[...]) + _dot(yc_ref[...], wob_ref[...])
    x3 = _swiglu_half_step(x2, g2_ref, wg_ref, wu_ref, wd_ref)
    gate = jax.nn.sigmoid(_dot(_rms(x3, gp_ref[...]).astype(BF16), wpg_ref[...]))
    out_ref[...] = x3 + gate * _dot(p_ref[...].astype(BF16), wpp_ref[...])


def _attn_kernel(qt_ref, k_ref, vt_ref, lq1_ref, lk1_ref, lq2_ref, lk2_ref, on_ref, o_ref,
                 qm, s_buf, mx_buf, p_buf, a_buf, m_sc, acc, *, lam_init):
    tq, tk = Q_TILE, KV_TILE
    i = pl.program_id(1)
    qt = qt_ref[...]
    feat = lax.broadcasted_iota(jnp.int32, qt.shape, 0)
    zero = jnp.zeros_like(qt)
    qm[:, 0:tq] = jnp.where(feat < QK_DIM, qt, zero)
    qm[:, tq:2 * tq] = jnp.where(feat >= QK_DIM, qt, zero)
    m_sc[...] = jnp.full(m_sc.shape, -jnp.inf, F32)
    acc[...] = jnp.zeros(acc.shape, F32)

    def visible_rows(lanes, diag):
        return (lanes.start % tq) + LANE_BLOCK if diag else tk

    def scores(tile, lanes, diag):
        rows = visible_rows(lanes, diag)
        start = pl.multiple_of(tile * tk, tk)
        st = _dot(k_ref[pl.ds(start, rows), :], qm[:, lanes])
        if diag:
            below, edge = st[:rows - LANE_BLOCK], st[rows - LANE_BLOCK:]
            kpos = lax.broadcasted_iota(jnp.int32, edge.shape, 0)
            qpos = lax.broadcasted_iota(jnp.int32, edge.shape, 1)
            edge = jnp.where(kpos <= qpos, edge, -jnp.inf)
            s_buf[rows - LANE_BLOCK:rows, lanes] = edge
            mx = jnp.max(edge, axis=0, keepdims=True)
            if rows > LANE_BLOCK:
                s_buf[0:rows - LANE_BLOCK, lanes] = below
                mx = jnp.maximum(mx, jnp.max(below, axis=0, keepdims=True))
            mx_buf[:, lanes] = mx
        else:
            s_buf[:, lanes] = st
            mx_buf[:, lanes] = jnp.max(st, axis=0, keepdims=True)

    def softmax(lanes, diag):
        rows = visible_rows(lanes, diag)
        m_prev = m_sc[:, lanes]
        m_new = jnp.maximum(m_prev, mx_buf[:, lanes])
        alpha = jnp.exp2(m_prev - m_new)
        p_buf[0:rows, lanes] = jnp.exp2((s_buf[0:rows, lanes] - m_new).astype(BF16))
        if rows < tk:
            p_buf[rows:tk, lanes] = jnp.zeros((tk - rows, LANE_BLOCK), BF16)
        a_buf[:, lanes] = alpha
        m_sc[:, lanes] = m_new

    def value(tile, lanes):
        start = pl.multiple_of(tile * tk, tk)
        acc[:, lanes] = (acc[:, lanes] * a_buf[:, lanes]
                         + _dot(vt_ref[:, pl.ds(start, tk)], p_buf[:, lanes]))

    def step(value_tile=None, softmax_diag=None, scores_tile=None, scores_diag=False):
        for blk in range(2 * tq // LANE_BLOCK):
            lanes = slice(blk * LANE_BLOCK, (blk + 1) * LANE_BLOCK)
            if value_tile is not None:
                value(value_tile, lanes)
            if softmax_diag is not None:
                softmax(lanes, softmax_diag)
            if scores_tile is not None:
                scores(scores_tile, lanes, scores_diag)

    step(scores_tile=i, scores_diag=True)

    @pl.when(i == 0)
    def _():
        step(softmax_diag=True)
        step(value_tile=i)

    @pl.when(i > 0)
    def _():
        step(softmax_diag=True, scores_tile=0)

        def body(n, carry):
            step(value_tile=jnp.where(n == 2, i, n - 3), softmax_diag=False, scores_tile=n - 1)
            return carry

        lax.fori_loop(2, i + 1, body, 0)
        step(value_tile=jnp.where(i == 1, i, i - 2), softmax_diag=False)
        step(value_tile=i - 1)

    lam = (jnp.exp(jnp.sum(lq1_ref[...] * lk1_ref[...], axis=-1, keepdims=True))
           - jnp.exp(jnp.sum(lq2_ref[...] * lk2_ref[...], axis=-1, keepdims=True)) + lam_init)
    num, den = acc[0:HEAD_DIM, :], acc[HEAD_DIM:HEAD_DIM + 1, :]
    ot = num[:, 0:tq] / den[:, 0:tq] - lam * (num[:, tq:2 * tq] / den[:, tq:2 * tq])
    ot = ot * lax.rsqrt(jnp.mean(ot * ot, axis=0, keepdims=True) + EPS) * on_ref[...]
    o_ref[...] = (ot * (1.0 - lam_init)).T.astype(BF16)


def _resident(a):
    if isinstance(a, tuple):
        _, block_shape, block_index = a
        return pl.BlockSpec(block_shape, lambda *_: block_index, pipeline_mode=pl.Buffered(1))
    return pl.BlockSpec(a.shape, lambda *_: (0,) * a.ndim, pipeline_mode=pl.Buffered(1))


def _operand(a):
    return a[0] if isinstance(a, tuple) else a


def _layer(stacked, layer):
    return (stacked, (None,) + stacked.shape[1:], (layer, 0, 0))


def _rows(width, tm=ROW_TILE):
    return pl.BlockSpec((tm, width), lambda i: (i, 0))


def _pre_call(x, g1, wg, wu, wd, gm, win, qn, kn, cw, cb, cn, ind):
    s = x.shape[0]
    row_major = pl.BlockSpec((N_HEADS, ROW_TILE, HEAD_DIM), lambda i: (0, i, 0))
    feat_major = pl.BlockSpec((N_HEADS, HEAD_DIM, ROW_TILE), lambda i: (0, 0, i))
    v_major = pl.BlockSpec((N_HEADS, V_ROWS, ROW_TILE), lambda i: (0, 0, i))
    consts = (g1, wg, wu, wd, gm, win, qn, kn, cw, cb, cn, ind)
    return pl.pallas_call(
        _pre_kernel,
        grid=(s // ROW_TILE,),
        in_specs=[_rows(D_MODEL)] + [_resident(c) for c in consts],
        out_specs=[_rows(D_MODEL), feat_major, row_major, v_major, _rows(CONV_WIDTH)],
        out_shape=[jax.ShapeDtypeStruct((s, D_MODEL), F32),
                   jax.ShapeDtypeStruct((N_HEADS, HEAD_DIM, s), BF16),
                   jax.ShapeDtypeStruct((N_HEADS, s, HEAD_DIM), BF16),
                   jax.ShapeDtypeStruct((N_HEADS, V_ROWS, s), BF16),
                   jax.ShapeDtypeStruct((s, CONV_WIDTH), BF16)],
        scratch_shapes=[pltpu.VMEM((ROW_TILE + CARRY_ROWS, CONV_WIDTH), F32)],
        compiler_params=pltpu.CompilerParams(
            dimension_semantics=("arbitrary",), vmem_limit_bytes=VMEM_LIMIT_BYTES),
        name="pre_mix",
    )(x, *map(_operand, consts))


def _post_call(x1, o, yc, p, layer, woa, wob, g2, wg, wu, wd, gp, wpg, wpp):
    s = x1.shape[0]
    consts = (woa, wob, g2, wg, wu, wd, gp, wpg, wpp)
    p_rows = pl.BlockSpec((None, None, ROW_TILE, D_PLE), lambda i: (layer, 0, i, 0))
    return pl.pallas_call(
        _post_kernel,
        grid=(s // ROW_TILE,),
        in_specs=[_rows(D_MODEL), _rows(ATTN_WIDTH), _rows(CONV_WIDTH), p_rows]
        + [_resident(c) for c in consts],
        out_specs=_rows(D_MODEL),
        out_shape=jax.ShapeDtypeStruct((s, D_MODEL), F32),
        compiler_params=pltpu.CompilerParams(
            dimension_semantics=("parallel",), vmem_limit_bytes=VMEM_LIMIT_BYTES),
        name="post_mix",
    )(x1, o, yc, p, *map(_operand, consts))


def _attn_call(qt, k, vt, lq1, lk1, lq2, lk2, on, lam_init):
    s = k.shape[1]
    qt_spec = pl.BlockSpec((None, HEAD_DIM, Q_TILE), lambda h, i: (h, 0, i))
    k_spec = pl.BlockSpec((None, s, HEAD_DIM), lambda h, i: (h, 0, 0))
    vt_spec = pl.BlockSpec((None, V_ROWS, s), lambda h, i: (h, 0, 0))
    vec = pl.BlockSpec((1, QK_DIM), lambda h, i: (0, 0))
    return pl.pallas_call(
        functools.partial(_attn_kernel, lam_init=lam_init),
        grid=(N_HEADS, s // Q_TILE),
        in_specs=[qt_spec, k_spec, vt_spec, vec, vec, vec, vec,
                  pl.BlockSpec((HEAD_DIM, 1), lambda h, i: (0, 0))],
        out_specs=pl.BlockSpec((Q_TILE, HEAD_DIM), lambda h, i: (i, h)),
        out_shape=jax.ShapeDtypeStruct((s, ATTN_WIDTH), BF16),
        scratch_shapes=[pltpu.VMEM((HEAD_DIM, 2 * Q_TILE), BF16),
                        pltpu.VMEM((KV_TILE, 2 * Q_TILE), F32),
                        pltpu.VMEM((1, 2 * Q_TILE), F32),
                        pltpu.VMEM((KV_TILE, 2 * Q_TILE), BF16),
                        pltpu.VMEM((1, 2 * Q_TILE), F32),
                        pltpu.VMEM((1, 2 * Q_TILE), F32),
                        pltpu.VMEM((V_ROWS, 2 * Q_TILE), F32)],
        compiler_params=pltpu.CompilerParams(
            dimension_semantics=("parallel", "parallel"), vmem_limit_bytes=VMEM_LIMIT_BYTES),
        name="diff_attn",
    )(qt, k, vt, lq1, lk1, lq2, lk2, on)


def kernel(x, p, ffn1_norm, ffn1_w_gate, ffn1_w_up, ffn1_w_down, mix_norm, w_in, q_norm, k_norm,
           lambda_q1, lambda_k1, lambda_q2, lambda_k2, attn_out_norm, conv_w, conv_b, conv_out_norm,
           w_out, ffn2_norm, ffn2_w_gate, ffn2_w_up, ffn2_w_down, ple_norm, ple_w_gate, ple_w_proj):
    b, s, _ = x.shape
    depth = p.shape[0]
    assert b == 1 and s % Q_TILE == 0 and s % ROW_TILE == 0 and Q_TILE == KV_TILE
    group = jnp.arange(MXU_TILE, dtype=jnp.int32) // GROUP_DIM
    ind = (group[:, None] == group[None, :]).astype(BF16)

    def row(a):
        return a.reshape(1, -1).astype(F32)

    def tiled(a):
        return jnp.tile(a.astype(F32), ATTN_WIDTH // a.shape[0]).reshape(1, ATTN_WIDTH)

    wg1, wu1, wd1, win, wo, wg2, wu2, wd2, wpg, wpp = (
        w.astype(BF16) for w in (ffn1_w_gate, ffn1_w_up, ffn1_w_down, w_in, w_out,
                                 ffn2_w_gate, ffn2_w_up, ffn2_w_down, ple_w_gate, ple_w_proj))
    xs = x.reshape(s, D_MODEL)
    for i in range(depth):
        lam_init = 0.8 - 0.6 * math.exp(-0.3 * i)
        x1, qt, k, vt, yc = _pre_call(
            xs, row(ffn1_norm[i]), _layer(wg1, i), _layer(wu1, i), _layer(wd1, i), row(mix_norm[i]),
            _layer(win, i), tiled(q_norm[i]), tiled(k_norm[i]), conv_w[i].astype(F32),
            row(conv_b[i]), row(conv_out_norm[i]), ind)
        o = _attn_call(qt, k, vt, row(lambda_q1[i]), row(lambda_k1[i]), row(lambda_q2[i]),
                       row(lambda_k2[i]), attn_out_norm[i].astype(F32).reshape(HEAD_DIM, 1), lam_init)
        wo_half = (None, ATTN_WIDTH, D_MODEL)
        xs = _post_call(
            x1, o, yc, p, i, (wo, wo_half, (i, 0, 0)), (wo, wo_half, (i, 1, 0)),
            row(ffn2_norm[i]), _layer(wg2, i), _layer(wu2, i), _layer(wd2, i), row(ple_norm[i]),
            _layer(wpg, i), _layer(wpp, i))
    return xs.reshape(b, s, D_MODEL)
```

```python
import functools
import math

import jax
import jax.numpy as jnp
from jax import lax
from jax.experimental import pallas as pl
from jax.experimental.pallas import tpu as pltpu

D_MODEL = 1024
D_FF = 2816
D_PLE = 256
ATTN_WIDTH = 512
CONV_WIDTH = 512
N_HEADS = 4
HEAD_DIM = 128
V_ROWS = HEAD_DIM + 16
QK_DIM = 64
GROUP_DIM = 64
CONV_K = 3
EPS = 1e-6

ROW_TILE = 512
Q_TILE = 2048
KV_TILE = 1024
MXU_TILE = 256
LANE_BLOCK = MXU_TILE
CARRY_ROWS = 8
VMEM_LIMIT_BYTES = 56 * 1024 * 1024

BF16 = jnp.bfloat16
F32 = jnp.float32


def _dot(a, b):
    return jnp.dot(a, b, preferred_element_type=F32)


def _rms(x, g):
    return x * lax.rsqrt(jnp.mean(x * x, axis=-1, keepdims=True) + EPS) * g


def _group_rms(x, g, ind):
    sq = x * x
    hi = sq.astype(BF16)
    lo = (sq - hi.astype(F32)).astype(BF16)
    sums = [_dot(hi[:, c:c + MXU_TILE], ind) + _dot(lo[:, c:c + MXU_TILE], ind)
            for c in range(0, x.shape[1], MXU_TILE)]
    ms = jnp.concatenate(sums, axis=1) * (1.0 / GROUP_DIM)
    return x * lax.rsqrt(ms + EPS) * g


def _swiglu_half_step(x, g_ref, wg_ref, wu_ref, wd_ref):
    n = _rms(x, g_ref[...]).astype(BF16)
    gate = _dot(n, wg_ref[...])
    up = _dot(n, wu_ref[...])
    h = (gate * jax.nn.sigmoid(gate) * up).astype(BF16)
    return x + 0.5 * _dot(h, wd_ref[...])


def _pre_kernel(x_ref, g1_ref, wg_ref, wu_ref, wd_ref, gm_ref, win_ref, qn_ref, kn_ref,
                cw_ref, cb_ref, cn_ref, ind_ref,
                x1_ref, q_ref, k_ref, v_ref, yc_ref, u_ext):
    tm = x_ref.shape[0]
    x1 = _swiglu_half_step(x_ref[...], g1_ref, wg_ref, wu_ref, wd_ref)
    x1_ref[...] = x1
    h = _rms(x1, gm_ref[...]).astype(BF16)
    ind = ind_ref[...]

    def z_cols(section):
        return _dot(h, win_ref[:, section * ATTN_WIDTH:(section + 1) * ATTN_WIDTH])

    q = _group_rms(z_cols(0), qn_ref[...], ind) * (math.log2(math.e) / math.sqrt(QK_DIM))
    k = _group_rms(z_cols(1), kn_ref[...], ind)
    v = z_cols(2)
    for hd in range(N_HEADS):
        cols = slice(hd * HEAD_DIM, (hd + 1) * HEAD_DIM)
        q_ref[hd] = q[:, cols].T.astype(BF16)
        k_ref[hd] = k[:, cols].astype(BF16)
        v_ref[hd, 0:HEAD_DIM, :] = v[:, cols].T.astype(BF16)
        v_ref[hd, HEAD_DIM:V_ROWS, :] = jnp.ones((V_ROWS - HEAD_DIM, tm), BF16)

    @pl.when(pl.program_id(0) == 0)
    def _():
        u_ext[0:CARRY_ROWS, :] = jnp.zeros((CARRY_ROWS, CONV_WIDTH), F32)

    u = z_cols(4) * z_cols(5)
    u_ext[CARRY_ROWS:CARRY_ROWS + tm, :] = u
    conv = (cw_ref[0:1, :] * u_ext[pl.ds(CARRY_ROWS - 2, tm), :]
            + cw_ref[1:2, :] * u_ext[pl.ds(CARRY_ROWS - 1, tm), :]
            + cw_ref[2:3, :] * u + cb_ref[...])
    u_ext[0:CARRY_ROWS, :] = u_ext[tm:tm + CARRY_ROWS, :]
    yc = z_cols(3) * conv
    yc_ref[...] = _group_rms(yc, cn_ref[...], ind).astype(BF16)


def _post_kernel(x1_ref, o_ref, yc_ref, p_ref, woa_ref, wob_ref, g2_ref, wg_ref, wu_ref, wd_ref,
                 gp_ref, wpg_ref, wpp_ref, out_ref):
    x2 = x1_ref[...] + _dot(o_ref[...], woa_ref[...]) + _dot(yc_ref[...], wob_ref[...])
    x3 = _swiglu_half_step(x2, g2_ref, wg_ref, wu_ref, wd_ref)
    gate = jax.nn.sigmoid(_dot(_rms(x3, gp_ref[...]).astype(BF16), wpg_ref[...]))
    out_ref[...] = x3 + gate * _dot(p_ref[...].astype(BF16), wpp_ref[...])


def _attn_kernel(qt_ref, k_ref, vt_ref, lq1_ref, lk1_ref, lq2_ref, lk2_ref, on_ref, o_ref,
                 qm, s_buf, mx_buf, p_buf, a_buf, m_sc, acc, *, lam_init):
    tq, tk = Q_TILE, KV_TILE
    i = pl.program_id(1)
    qt = qt_ref[...]
    feat = lax.broadcasted_iota(jnp.int32, qt.shape, 0)
    zero = jnp.zeros_like(qt)
    qm[:, 0:tq] = jnp.where(feat < QK_DIM, qt, zero)
    qm[:, tq:2 * tq] = jnp.where(feat >= QK_DIM, qt, zero)
    m_sc[...] = jnp.full(m_sc.shape, -jnp.inf, F32)
    acc[...] = jnp.zeros(acc.shape, F32)

    def visible_rows(lanes, diag):
        if diag is None:
            return tk
        return max(0, min(tk, (lanes.start % tq) + LANE_BLOCK - diag * tk))

    def straddles(lanes, diag):
        return diag is not None and 0 < (lanes.start % tq) + LANE_BLOCK - diag * tk <= tk

    def scores(tile, lanes, diag):
        rows = visible_rows(lanes, diag)
        if rows == 0:
            return
        start = pl.multiple_of(tile * tk, tk)
        st = _dot(k_ref[pl.ds(start, rows), :], qm[:, lanes])
        if straddles(lanes, diag):
            below, edge = st[:rows - LANE_BLOCK], st[rows - LANE_BLOCK:]
            kpos = lax.broadcasted_iota(jnp.int32, edge.shape, 0)
            qpos = lax.broadcasted_iota(jnp.int32, edge.shape, 1)
            edge = jnp.where(kpos <= qpos, edge, -jnp.inf)
            s_buf[rows - LANE_BLOCK:rows, lanes] = edge
            mx = jnp.max(edge, axis=0, keepdims=True)
            if rows > LANE_BLOCK:
                s_buf[0:rows - LANE_BLOCK, lanes] = below
                mx = jnp.maximum(mx, jnp.max(below, axis=0, keepdims=True))
            mx_buf[:, lanes] = mx
        else:
            s_buf[0:rows, lanes] = st
            mx_buf[:, lanes] = jnp.max(st, axis=0, keepdims=True)

    def softmax(lanes, diag):
        rows = visible_rows(lanes, diag)
        if rows == 0:
            return
        m_prev = m_sc[:, lanes]
        m_new = jnp.maximum(m_prev, mx_buf[:, lanes])
        alpha = jnp.exp2(m_prev - m_new)
        p_buf[0:rows, lanes] = jnp.exp2((s_buf[0:rows, lanes] - m_new).astype(BF16))
        if rows < tk:
            p_buf[rows:tk, lanes] = jnp.zeros((tk - rows, LANE_BLOCK), BF16)
        a_buf[:, lanes] = alpha
        m_sc[:, lanes] = m_new

    def value(tile, lanes, diag):
        if visible_rows(lanes, diag) == 0:
            return
        start = pl.multiple_of(tile * tk, tk)
        acc[:, lanes] = (acc[:, lanes] * a_buf[:, lanes]
                         + _dot(vt_ref[:, pl.ds(start, tk)], p_buf[:, lanes]))

    def step(value_of=None, softmax_of=None, scores_of=None):
        for blk in range(2 * tq // LANE_BLOCK):
            lanes = slice(blk * LANE_BLOCK, (blk + 1) * LANE_BLOCK)
            if value_of is not None:
                value(value_of[0], lanes, value_of[1])
            if softmax_of is not None:
                softmax(lanes, softmax_of[1])
            if scores_of is not None:
                scores(scores_of[0], lanes, scores_of[1])

    n_diag = tq // tk
    first_diag = i * n_diag
    n_items = n_diag + first_diag

    def item(j):
        if isinstance(j, int) and j < n_diag:
            return (first_diag + j, j)
        return (j - n_diag, None)

    def run_step(s, last_item):
        def stage(j):
            return item(j) if 0 <= j <= last_item else None
        step(value_of=stage(s - 2), softmax_of=stage(s - 1), scores_of=stage(s))

    for s in range(n_diag):
        run_step(s, n_diag - 1)

    @pl.when(i == 0)
    def _():
        run_step(n_diag, n_diag - 1)
        run_step(n_diag + 1, n_diag - 1)

    @pl.when(i > 0)
    def _():
        run_step(n_diag, n_diag + 1)
        run_step(n_diag + 1, n_diag + 1)

        def body(s, carry):
            step(value_of=item(s - 2), softmax_of=item(s - 1), scores_of=item(s))
            return carry

        lax.fori_loop(n_diag + 2, n_items, body, 0)
        step(value_of=item(n_items - 2), softmax_of=item(n_items - 1))
        step(value_of=item(n_items - 1))

    lam = (jnp.exp(jnp.sum(lq1_ref[...] * lk1_ref[...], axis=-1, keepdims=True))
           - jnp.exp(jnp.sum(lq2_ref[...] * lk2_ref[...], axis=-1, keepdims=True)) + lam_init)
    num, den = acc[0:HEAD_DIM, :], acc[HEAD_DIM:HEAD_DIM + 1, :]
    ot = num[:, 0:tq] / den[:, 0:tq] - lam * (num[:, tq:2 * tq] / den[:, tq:2 * tq])
    ot = ot * lax.rsqrt(jnp.mean(ot * ot, axis=0, keepdims=True) + EPS) * on_ref[...]
    o_ref[...] = (ot * (1.0 - lam_init)).T.astype(BF16)


def _resident(a):
    if isinstance(a, tuple):
        _, block_shape, block_index = a
        return pl.BlockSpec(block_shape, lambda *_: block_index, pipeline_mode=pl.Buffered(1))
    return pl.BlockSpec(a.shape, lambda *_: (0,) * a.ndim, pipeline_mode=pl.Buffered(1))


def _operand(a):
    return a[0] if isinstance(a, tuple) else a


def _layer(stacked, layer):
    return (stacked, (None,) + stacked.shape[1:], (layer, 0, 0))


def _rows(width, tm=ROW_TILE):
    return pl.BlockSpec((tm, width), lambda i: (i, 0))


def _pre_call(x, g1, wg, wu, wd, gm, win, qn, kn, cw, cb, cn, ind):
    s = x.shape[0]
    row_major = pl.BlockSpec((N_HEADS, ROW_TILE, HEAD_DIM), lambda i: (0, i, 0))
    feat_major = pl.BlockSpec((N_HEADS, HEAD_DIM, ROW_TILE), lambda i: (0, 0, i))
    v_major = pl.BlockSpec((N_HEADS, V_ROWS, ROW_TILE), lambda i: (0, 0, i))
    consts = (g1, wg, wu, wd, gm, win, qn, kn, cw, cb, cn, ind)
    return pl.pallas_call(
        _pre_kernel,
        grid=(s // ROW_TILE,),
        in_specs=[_rows(D_MODEL)] + [_resident(c) for c in consts],
        out_specs=[_rows(D_MODEL), feat_major, row_major, v_major, _rows(CONV_WIDTH)],
        out_shape=[jax.ShapeDtypeStruct((s, D_MODEL), F32),
                   jax.ShapeDtypeStruct((N_HEADS, HEAD_DIM, s), BF16),
                   jax.ShapeDtypeStruct((N_HEADS, s, HEAD_DIM), BF16),
                   jax.ShapeDtypeStruct((N_HEADS, V_ROWS, s), BF16),
                   jax.ShapeDtypeStruct((s, CONV_WIDTH), BF16)],
        scratch_shapes=[pltpu.VMEM((ROW_TILE + CARRY_ROWS, CONV_WIDTH), F32)],
        compiler_params=pltpu.CompilerParams(
            dimension_semantics=("arbitrary",), vmem_limit_bytes=VMEM_LIMIT_BYTES),
        name="pre_mix",
    )(x, *map(_operand, consts))


def _post_call(x1, o, yc, p, layer, woa, wob, g2, wg, wu, wd, gp, wpg, wpp):
    s = x1.shape[0]
    consts = (woa, wob, g2, wg, wu, wd, gp, wpg, wpp)
    p_rows = pl.BlockSpec((None, None, ROW_TILE, D_PLE), lambda i: (layer, 0, i, 0))
    return pl.pallas_call(
        _post_kernel,
        grid=(s // ROW_TILE,),
        in_specs=[_rows(D_MODEL), _rows(ATTN_WIDTH), _rows(CONV_WIDTH), p_rows]
        + [_resident(c) for c in consts],
        out_specs=_rows(D_MODEL),
        out_shape=jax.ShapeDtypeStruct((s, D_MODEL), F32),
        compiler_params=pltpu.CompilerParams(
            dimension_semantics=("parallel",), vmem_limit_bytes=VMEM_LIMIT_BYTES),
        name="post_mix",
    )(x1, o, yc, p, *map(_operand, consts))


def _attn_call(qt, k, vt, lq1, lk1, lq2, lk2, on, lam_init):
    s = k.shape[1]
    qt_spec = pl.BlockSpec((None, HEAD_DIM, Q_TILE), lambda h, i: (h, 0, i))
    k_spec = pl.BlockSpec((None, s, HEAD_DIM), lambda h, i: (h, 0, 0))
    vt_spec = pl.BlockSpec((None, V_ROWS, s), lambda h, i: (h, 0, 0))
    vec = pl.BlockSpec((1, QK_DIM), lambda h, i: (0, 0))
    return pl.pallas_call(
        functools.partial(_attn_kernel, lam_init=lam_init),
        grid=(N_HEADS, s // Q_TILE),
        in_specs=[qt_spec, k_spec, vt_spec, vec, vec, vec, vec,
                  pl.BlockSpec((HEAD_DIM, 1), lambda h, i: (0, 0))],
        out_specs=pl.BlockSpec((Q_TILE, HEAD_DIM), lambda h, i: (i, h)),
        out_shape=jax.ShapeDtypeStruct((s, ATTN_WIDTH), BF16),
        scratch_shapes=[pltpu.VMEM((HEAD_DIM, 2 * Q_TILE), BF16),
                        pltpu.VMEM((KV_TILE, 2 * Q_TILE), F32),
                        pltpu.VMEM((1, 2 * Q_TILE), F32),
                        pltpu.VMEM((KV_TILE, 2 * Q_TILE), BF16),
                        pltpu.VMEM((1, 2 * Q_TILE), F32),
                        pltpu.VMEM((1, 2 * Q_TILE), F32),
                        pltpu.VMEM((V_ROWS, 2 * Q_TILE), F32)],
        compiler_params=pltpu.CompilerParams(
            dimension_semantics=("parallel", "parallel"), vmem_limit_bytes=VMEM_LIMIT_BYTES),
        name="diff_attn",
    )(qt, k, vt, lq1, lk1, lq2, lk2, on)


def kernel(x, p, ffn1_norm, ffn1_w_gate, ffn1_w_up, ffn1_w_down, mix_norm, w_in, q_norm, k_norm,
           lambda_q1, lambda_k1, lambda_q2, lambda_k2, attn_out_norm, conv_w, conv_b, conv_out_norm,
           w_out, ffn2_norm, ffn2_w_gate, ffn2_w_up, ffn2_w_down, ple_norm, ple_w_gate, ple_w_proj):
    b, s, _ = x.shape
    depth = p.shape[0]
    assert b == 1 and s % Q_TILE == 0 and s % ROW_TILE == 0 and Q_TILE % KV_TILE == 0
    group = jnp.arange(MXU_TILE, dtype=jnp.int32) // GROUP_DIM
    ind = (group[:, None] == group[None, :]).astype(BF16)

    def row(a):
        return a.reshape(1, -1).astype(F32)

    def tiled(a):
        return jnp.tile(a.astype(F32), ATTN_WIDTH // a.shape[0]).reshape(1, ATTN_WIDTH)

    wg1, wu1, wd1, win, wo, wg2, wu2, wd2, wpg, wpp = (
        w.astype(BF16) for w in (ffn1_w_gate, ffn1_w_up, ffn1_w_down, w_in, w_out,
                                 ffn2_w_gate, ffn2_w_up, ffn2_w_down, ple_w_gate, ple_w_proj))
    xs = x.reshape(s, D_MODEL)
    for i in range(depth):
        lam_init = 0.8 - 0.6 * math.exp(-0.3 * i)
        x1, qt, k, vt, yc = _pre_call(
            xs, row(ffn1_norm[i]), _layer(wg1, i), _layer(wu1, i), _layer(wd1, i), row(mix_norm[i]),
            _layer(win, i), tiled(q_norm[i]), tiled(k_norm[i]), conv_w[i].astype(F32),
            row(conv_b[i]), row(conv_out_norm[i]), ind)
        o = _attn_call(qt, k, vt, row(lambda_q1[i]), row(lambda_k1[i]), row(lambda_q2[i]),
                       row(lambda_k2[i]), attn_out_norm[i].astype(F32).reshape(HEAD_DIM, 1), lam_init)
        wo_half = (None, ATTN_WIDTH, D_MODEL)
        xs = _post_call(
            x1, o, yc, p, i, (wo, wo_half, (i, 0, 0)), (wo, wo_half, (i, 1, 0)),
            row(ffn2_norm[i]), _layer(wg2, i), _layer(wu2, i), _layer(wd2, i), row(ple_norm[i]),
            _layer(wpg, i), _layer(wpp, i))
    return xs.reshape(b, s, D_MODEL)
```

```python
import functools
import math

import jax
import jax.numpy as jnp
from jax import lax
from jax.experimental import pallas as pl
from jax.experimental.pallas import tpu as pltpu

D_MODEL = 1024
D_FF = 2816
D_PLE = 256
ATTN_WIDTH = 512
CONV_WIDTH = 512
N_HEADS = 4
HEAD_DIM = 128
BF16_TILE_ROWS = 16
V_ROWS = HEAD_DIM + BF16_TILE_ROWS
QK_DIM = 64
GROUP_DIM = 64
CONV_K = 3
EPS = 1e-6

ROW_TILE = 512
Q_TILE = 2048
KV_TILE = 1024
MXU_TILE = 256
LANE_BLOCK = MXU_TILE
CARRY_ROWS = 8
V7X_VMEM_BYTES = 64 * 1024 * 1024
VMEM_LIMIT_BYTES = V7X_VMEM_BYTES * 7 // 8

BF16 = jnp.bfloat16
F32 = jnp.float32


def _dot(a, b):
    return jnp.dot(a, b, preferred_element_type=F32)


def _rms(x, g):
    return x * lax.rsqrt(jnp.mean(x * x, axis=-1, keepdims=True) + EPS) * g


def _group_rms(x, g, ind):
    sq = x * x
    hi = sq.astype(BF16)
    lo = (sq - hi.astype(F32)).astype(BF16)
    sums = [_dot(hi[:, c:c + MXU_TILE], ind) + _dot(lo[:, c:c + MXU_TILE], ind)
            for c in range(0, x.shape[1], MXU_TILE)]
    ms = jnp.concatenate(sums, axis=1) * (1.0 / GROUP_DIM)
    return x * lax.rsqrt(ms + EPS) * g


def _swiglu_half_step(x, g_ref, wg_ref, wu_ref, wd_ref):
    n = _rms(x, g_ref[...]).astype(BF16)
    gate = _dot(n, wg_ref[...])
    up = _dot(n, wu_ref[...])
    h = (gate * jax.nn.sigmoid(gate) * up).astype(BF16)
    return x + 0.5 * _dot(h, wd_ref[...])


def _pre_kernel(x_ref, g1_ref, wg_ref, wu_ref, wd_ref, gm_ref, win_ref, qn_ref, kn_ref,
                cw_ref, cb_ref, cn_ref, ind_ref,
                x1_ref, q_ref, k_ref, v_ref, yc_ref, u_ext):
    tm = x_ref.shape[0]
    x1 = _swiglu_half_step(x_ref[...], g1_ref, wg_ref, wu_ref, wd_ref)
    x1_ref[...] = x1
    h = _rms(x1, gm_ref[...]).astype(BF16)
    ind = ind_ref[...]

    def z_cols(section):
        return _dot(h, win_ref[:, section * ATTN_WIDTH:(section + 1) * ATTN_WIDTH])

    q = _group_rms(z_cols(0), qn_ref[...], ind) * (math.log2(math.e) / math.sqrt(QK_DIM))
    k = _group_rms(z_cols(1), kn_ref[...], ind)
    v = z_cols(2)
    for hd in range(N_HEADS):
        cols = slice(hd * HEAD_DIM, (hd + 1) * HEAD_DIM)
        q_ref[hd] = q[:, cols].T.astype(BF16)
        k_ref[hd] = k[:, cols].astype(BF16)
        v_ref[hd, 0:HEAD_DIM, :] = v[:, cols].T.astype(BF16)
        v_ref[hd, HEAD_DIM:V_ROWS, :] = jnp.ones((V_ROWS - HEAD_DIM, tm), BF16)

    @pl.when(pl.program_id(0) == 0)
    def _():
        u_ext[0:CARRY_ROWS, :] = jnp.zeros((CARRY_ROWS, CONV_WIDTH), F32)

    u = z_cols(4) * z_cols(5)
    u_ext[CARRY_ROWS:CARRY_ROWS + tm, :] = u
    conv = (cw_ref[0:1, :] * u_ext[pl.ds(CARRY_ROWS - 2, tm), :]
            + cw_ref[1:2, :] * u_ext[pl.ds(CARRY_ROWS - 1, tm), :]
            + cw_ref[2:3, :] * u + cb_ref[...])
    u_ext[0:CARRY_ROWS, :] = u_ext[tm:tm + CARRY_ROWS, :]
    yc = z_cols(3) * conv
    yc_ref[...] = _group_rms(yc, cn_ref[...], ind).astype(BF16)


def _post_kernel(x1_ref, o_ref, yc_ref, p_ref, woa_ref, wob_ref, g2_ref, wg_ref, wu_ref, wd_ref,
                 gp_ref, wpg_ref, wpp_ref, out_ref):
    x2 = x1_ref[...] + _dot(o_ref[...], woa_ref[...]) + _dot(yc_ref[...], wob_ref[...])
    x3 = _swiglu_half_step(x2, g2_ref, wg_ref, wu_ref, wd_ref)
    gate = jax.nn.sigmoid(_dot(_rms(x3, gp_ref[...]).astype(BF16), wpg_ref[...]))
    out_ref[...] = x3 + gate * _dot(p_ref[...].astype(BF16), wpp_ref[...])


def _attn_kernel(qt_ref, k_ref, vt_ref, lq1_ref, lk1_ref, lq2_ref, lk2_ref, on_ref, o_ref,
                 qm, s_buf, mx_buf, p_buf, a_buf, m_sc, acc, *, lam_init):
    tq, tk = Q_TILE, KV_TILE
    i = pl.program_id(1)
    qt = qt_ref[...]
    feat = lax.broadcasted_iota(jnp.int32, qt.shape, 0)
    zero = jnp.zeros_like(qt)
    qm[:, 0:tq] = jnp.where(feat < QK_DIM, qt, zero)
    qm[:, tq:2 * tq] = jnp.where(feat >= QK_DIM, qt, zero)
    m_sc[...] = jnp.full(m_sc.shape, -jnp.inf, F32)
    acc[...] = jnp.zeros(acc.shape, F32)

    def visible_rows(lanes, diag):
        if diag is None:
            return tk
        return max(0, min(tk, (lanes.start % tq) + LANE_BLOCK - diag * tk))

    def straddles(lanes, diag):
        return diag is not None and 0 < (lanes.start % tq) + LANE_BLOCK - diag * tk <= tk

    def scores(tile, lanes, diag):
        rows = visible_rows(lanes, diag)
        if rows == 0:
            return
        start = pl.multiple_of(tile * tk, tk)
        st = _dot(k_ref[pl.ds(start, rows), :], qm[:, lanes])
        if straddles(lanes, diag):
            below, edge = st[:rows - LANE_BLOCK], st[rows - LANE_BLOCK:]
            kpos = lax.broadcasted_iota(jnp.int32, edge.shape, 0)
            qpos = lax.broadcasted_iota(jnp.int32, edge.shape, 1)
            edge = jnp.where(kpos <= qpos, edge, -jnp.inf)
            s_buf[rows - LANE_BLOCK:rows, lanes] = edge
            mx = jnp.max(edge, axis=0, keepdims=True)
            if rows > LANE_BLOCK:
                s_buf[0:rows - LANE_BLOCK, lanes] = below
                mx = jnp.maximum(mx, jnp.max(below, axis=0, keepdims=True))
            mx_buf[:, lanes] = mx
        else:
            s_buf[0:rows, lanes] = st
            mx_buf[:, lanes] = jnp.max(st, axis=0, keepdims=True)

    def softmax(lanes, diag):
        rows = visible_rows(lanes, diag)
        if rows == 0:
            return
        m_prev = m_sc[:, lanes]
        m_new = jnp.maximum(m_prev, mx_buf[:, lanes])
        alpha = jnp.exp2(m_prev - m_new)
        p_buf[0:rows, lanes] = jnp.exp2((s_buf[0:rows, lanes] - m_new).astype(BF16))
        if rows < tk:
            p_buf[rows:tk, lanes] = jnp.zeros((tk - rows, LANE_BLOCK), BF16)
        a_buf[:, lanes] = alpha
        m_sc[:, lanes] = m_new

    def value(tile, lanes, diag):
        if visible_rows(lanes, diag) == 0:
            return
        start = pl.multiple_of(tile * tk, tk)
        acc[:, lanes] = (acc[:, lanes] * a_buf[:, lanes]
                         + _dot(vt_ref[:, pl.ds(start, tk)], p_buf[:, lanes]))

    def step(value_of=None, softmax_of=None, scores_of=None):
        for blk in range(2 * tq // LANE_BLOCK):
            lanes = slice(blk * LANE_BLOCK, (blk + 1) * LANE_BLOCK)
            if value_of is not None:
                value(value_of[0], lanes, value_of[1])
            if softmax_of is not None:
                softmax(lanes, softmax_of[1])
            if scores_of is not None:
                scores(scores_of[0], lanes, scores_of[1])

    n_diag = tq // tk
    first_diag = i * n_diag
    n_items = n_diag + first_diag

    def item(j):
        if isinstance(j, int) and j < n_diag:
            return (first_diag + j, j)
        return (j - n_diag, None)

    def run_step(s, last_item):
        def stage(j):
            return item(j) if 0 <= j <= last_item else None
        step(value_of=stage(s - 2), softmax_of=stage(s - 1), scores_of=stage(s))

    for s in range(n_diag):
        run_step(s, n_diag - 1)

    @pl.when(i == 0)
    def _():
        run_step(n_diag, n_diag - 1)
        run_step(n_diag + 1, n_diag - 1)

    @pl.when(i > 0)
    def _():
        run_step(n_diag, n_diag + 1)
        run_step(n_diag + 1, n_diag + 1)

        def body(s, carry):
            step(value_of=item(s - 2), softmax_of=item(s - 1), scores_of=item(s))
            return carry

        lax.fori_loop(n_diag + 2, n_items, body, 0)
        step(value_of=item(n_items - 2), softmax_of=item(n_items - 1))
        step(value_of=item(n_items - 1))

    lam = (jnp.exp(jnp.sum(lq1_ref[...] * lk1_ref[...], axis=-1, keepdims=True))
           - jnp.exp(jnp.sum(lq2_ref[...] * lk2_ref[...], axis=-1, keepdims=True)) + lam_init)
    num, den = acc[0:HEAD_DIM, :], acc[HEAD_DIM:HEAD_DIM + 1, :]
    ot = num[:, 0:tq] / den[:, 0:tq] - lam * (num[:, tq:2 * tq] / den[:, tq:2 * tq])
    ot = ot * lax.rsqrt(jnp.mean(ot * ot, axis=0, keepdims=True) + EPS) * on_ref[...]
    o_ref[...] = (ot * (1.0 - lam_init)).T.astype(BF16)


def _resident(a):
    if isinstance(a, tuple):
        _, block_shape, block_index = a
        return pl.BlockSpec(block_shape, lambda *_: block_index, pipeline_mode=pl.Buffered(1))
    return pl.BlockSpec(a.shape, lambda *_: (0,) * a.ndim, pipeline_mode=pl.Buffered(1))


def _operand(a):
    return a[0] if isinstance(a, tuple) else a


def _layer(stacked, layer):
    return (stacked, (None,) + stacked.shape[1:], (layer, 0, 0))


def _rows(width, tm=ROW_TILE):
    return pl.BlockSpec((tm, width), lambda i: (i, 0))


def _pre_call(x, g1, wg, wu, wd, gm, win, qn, kn, cw, cb, cn, ind):
    s = x.shape[0]
    row_major = pl.BlockSpec((N_HEADS, ROW_TILE, HEAD_DIM), lambda i: (0, i, 0))
    feat_major = pl.BlockSpec((N_HEADS, HEAD_DIM, ROW_TILE), lambda i: (0, 0, i))
    v_major = pl.BlockSpec((N_HEADS, V_ROWS, ROW_TILE), lambda i: (0, 0, i))
    consts = (g1, wg, wu, wd, gm, win, qn, kn, cw, cb, cn, ind)
    return pl.pallas_call(
        _pre_kernel,
        grid=(s // ROW_TILE,),
        in_specs=[_rows(D_MODEL)] + [_resident(c) for c in consts],
        out_specs=[_rows(D_MODEL), feat_major, row_major, v_major, _rows(CONV_WIDTH)],
        out_shape=[jax.ShapeDtypeStruct((s, D_MODEL), F32),
                   jax.ShapeDtypeStruct((N_HEADS, HEAD_DIM, s), BF16),
                   jax.ShapeDtypeStruct((N_HEADS, s, HEAD_DIM), BF16),
                   jax.ShapeDtypeStruct((N_HEADS, V_ROWS, s), BF16),
                   jax.ShapeDtypeStruct((s, CONV_WIDTH), BF16)],
        scratch_shapes=[pltpu.VMEM((ROW_TILE + CARRY_ROWS, CONV_WIDTH), F32)],
        compiler_params=pltpu.CompilerParams(
            dimension_semantics=("arbitrary",), vmem_limit_bytes=VMEM_LIMIT_BYTES),
        name="pre_mix",
    )(x, *map(_operand, consts))


def _post_call(x1, o, yc, p, layer, woa, wob, g2, wg, wu, wd, gp, wpg, wpp):
    s = x1.shape[0]
    consts = (woa, wob, g2, wg, wu, wd, gp, wpg, wpp)
    p_rows = pl.BlockSpec((None, None, ROW_TILE, D_PLE), lambda i: (layer, 0, i, 0))
    return pl.pallas_call(
        _post_kernel,
        grid=(s // ROW_TILE,),
        in_specs=[_rows(D_MODEL), _rows(ATTN_WIDTH), _rows(CONV_WIDTH), p_rows]
        + [_resident(c) for c in consts],
        out_specs=_rows(D_MODEL),
        out_shape=jax.ShapeDtypeStruct((s, D_MODEL), F32),
        compiler_params=pltpu.CompilerParams(
            dimension_semantics=("parallel",), vmem_limit_bytes=VMEM_LIMIT_BYTES),
        name="post_mix",
    )(x1, o, yc, p, *map(_operand, consts))


def _attn_call(qt, k, vt, lq1, lk1, lq2, lk2, on, lam_init):
    s = k.shape[1]
    qt_spec = pl.BlockSpec((None, HEAD_DIM, Q_TILE), lambda h, i: (h, 0, i))
    k_spec = pl.BlockSpec((None, s, HEAD_DIM), lambda h, i: (h, 0, 0))
    vt_spec = pl.BlockSpec((None, V_ROWS, s), lambda h, i: (h, 0, 0))
    vec = pl.BlockSpec((1, QK_DIM), lambda h, i: (0, 0))
    return pl.pallas_call(
        functools.partial(_attn_kernel, lam_init=lam_init),
        grid=(N_HEADS, s // Q_TILE),
        in_specs=[qt_spec, k_spec, vt_spec, vec, vec, vec, vec,
                  pl.BlockSpec((HEAD_DIM, 1), lambda h, i: (0, 0))],
        out_specs=pl.BlockSpec((Q_TILE, HEAD_DIM), lambda h, i: (i, h)),
        out_shape=jax.ShapeDtypeStruct((s, ATTN_WIDTH), BF16),
        scratch_shapes=[pltpu.VMEM((HEAD_DIM, 2 * Q_TILE), BF16),
                        pltpu.VMEM((KV_TILE, 2 * Q_TILE), F32),
                        pltpu.VMEM((1, 2 * Q_TILE), F32),
                        pltpu.VMEM((KV_TILE, 2 * Q_TILE), BF16),
                        pltpu.VMEM((1, 2 * Q_TILE), F32),
                        pltpu.VMEM((1, 2 * Q_TILE), F32),
                        pltpu.VMEM((V_ROWS, 2 * Q_TILE), F32)],
        compiler_params=pltpu.CompilerParams(
            dimension_semantics=("parallel", "parallel"), vmem_limit_bytes=VMEM_LIMIT_BYTES),
        name="diff_attn",
    )(qt, k, vt, lq1, lk1, lq2, lk2, on)


def kernel(x, p, ffn1_norm, ffn1_w_gate, ffn1_w_up, ffn1_w_down, mix_norm, w_in, q_norm, k_norm,
           lambda_q1, lambda_k1, lambda_q2, lambda_k2, attn_out_norm, conv_w, conv_b, conv_out_norm,
           w_out, ffn2_norm, ffn2_w_gate, ffn2_w_up, ffn2_w_down, ple_norm, ple_w_gate, ple_w_proj):
    b, s, _ = x.shape
    depth = p.shape[0]
    assert b == 1 and s % Q_TILE == 0 and s % ROW_TILE == 0 and Q_TILE % KV_TILE == 0
    group = jnp.arange(MXU_TILE, dtype=jnp.int32) // GROUP_DIM
    ind = (group[:, None] == group[None, :]).astype(BF16)

    def row(a):
        return a.reshape(1, -1).astype(F32)

    def tiled(a):
        return jnp.tile(a.astype(F32), ATTN_WIDTH // a.shape[0]).reshape(1, ATTN_WIDTH)

    wg1, wu1, wd1, win, wo, wg2, wu2, wd2, wpg, wpp = (
        w.astype(BF16) for w in (ffn1_w_gate, ffn1_w_up, ffn1_w_down, w_in, w_out,
                                 ffn2_w_gate, ffn2_w_up, ffn2_w_down, ple_w_gate, ple_w_proj))
    xs = x.reshape(s, D_MODEL)
    for i in range(depth):
        lam_init = 0.8 - 0.6 * math.exp(-0.3 * i)
        x1, qt, k, vt, yc = _pre_call(
            xs, row(ffn1_norm[i]), _layer(wg1, i), _layer(wu1, i), _layer(wd1, i), row(mix_norm[i]),
            _layer(win, i), tiled(q_norm[i]), tiled(k_norm[i]), conv_w[i].astype(F32),
            row(conv_b[i]), row(conv_out_norm[i]), ind)
        o = _attn_call(qt, k, vt, row(lambda_q1[i]), row(lambda_k1[i]), row(lambda_q2[i]),
                       row(lambda_k2[i]), attn_out_norm[i].astype(F32).reshape(HEAD_DIM, 1), lam_init)
        wo_half = (None, ATTN_WIDTH, D_MODEL)
        xs = _post_call(
            x1, o, yc, p, i, (wo, wo_half, (i, 0, 0)), (wo, wo_half, (i, 1, 0)),
            row(ffn2_norm[i]), _layer(wg2, i), _layer(wu2, i), _layer(wd2, i), row(ple_norm[i]),
            _layer(wpg, i), _layer(wpp, i))
    return xs.reshape(b, s, D_MODEL)
```

```python
import functools
import math

import jax
import jax.numpy as jnp
from jax import lax
from jax.experimental import pallas as pl
from jax.experimental.pallas import tpu as pltpu

D_MODEL = 1024
D_FF = 2816
D_PLE = 256
ATTN_WIDTH = 512
CONV_WIDTH = 512
N_HEADS = 4
HEAD_DIM = 128
BF16_TILE_ROWS = 16
V_ROWS = HEAD_DIM + BF16_TILE_ROWS
QK_DIM = 64
GROUP_DIM = 64
CONV_K = 3
EPS = 1e-6

ROW_TILE = 512
Q_TILE = 2048
KV_TILE = 1024
MXU_TILE = 256
LANE_BLOCK = MXU_TILE
CARRY_ROWS = 8
V7X_VMEM_BYTES = 64 * 1024 * 1024
VMEM_LIMIT_BYTES = V7X_VMEM_BYTES * 7 // 8

BF16 = jnp.bfloat16
F32 = jnp.float32


def _dot(a, b):
    return jnp.dot(a, b, preferred_element_type=F32)


def _rms(x, g):
    return x * lax.rsqrt(jnp.mean(x * x, axis=-1, keepdims=True) + EPS) * g


def _group_rms(x, g, ind):
    sq = x * x
    hi = sq.astype(BF16)
    lo = (sq - hi.astype(F32)).astype(BF16)
    sums = [_dot(hi[:, c:c + MXU_TILE], ind) + _dot(lo[:, c:c + MXU_TILE], ind)
            for c in range(0, x.shape[1], MXU_TILE)]
    ms = jnp.concatenate(sums, axis=1) * (1.0 / GROUP_DIM)
    return x * lax.rsqrt(ms + EPS) * g


def _swiglu_half_step(x, g_ref, wg_ref, wu_ref, wd_ref):
    n = _rms(x, g_ref[...]).astype(BF16)
    gate = _dot(n, wg_ref[...])
    up = _dot(n, wu_ref[...])
    h = (gate * jax.nn.sigmoid(gate) * up).astype(BF16)
    return x + 0.5 * _dot(h, wd_ref[...])


def _pre_kernel(x_ref, g1_ref, wg_ref, wu_ref, wd_ref, gm_ref, win_ref, qn_ref, kn_ref,
                cw_ref, cb_ref, cn_ref, ind_ref,
                x1_ref, q_ref, k_ref, v_ref, yc_ref, u_ext):
    tm = x_ref.shape[0]
    x1 = _swiglu_half_step(x_ref[...], g1_ref, wg_ref, wu_ref, wd_ref)
    x1_ref[...] = x1
    h = _rms(x1, gm_ref[...]).astype(BF16)
    ind = ind_ref[...]

    def z_cols(section):
        return _dot(h, win_ref[:, section * ATTN_WIDTH:(section + 1) * ATTN_WIDTH])

    q = _group_rms(z_cols(0), qn_ref[...], ind) * (math.log2(math.e) / math.sqrt(QK_DIM))
    k = _group_rms(z_cols(1), kn_ref[...], ind)
    v = z_cols(2)
    for hd in range(N_HEADS):
        cols = slice(hd * HEAD_DIM, (hd + 1) * HEAD_DIM)
        q_ref[hd] = q[:, cols].T.astype(BF16)
        k_ref[hd] = k[:, cols].astype(BF16)
        v_ref[hd, 0:HEAD_DIM, :] = v[:, cols].T.astype(BF16)
        v_ref[hd, HEAD_DIM:V_ROWS, :] = jnp.ones((V_ROWS - HEAD_DIM, tm), BF16)

    @pl.when(pl.program_id(0) == 0)
    def _():
        u_ext[0:CARRY_ROWS, :] = jnp.zeros((CARRY_ROWS, CONV_WIDTH), F32)

    u = z_cols(4) * z_cols(5)
    u_ext[CARRY_ROWS:CARRY_ROWS + tm, :] = u
    conv = (cw_ref[0:1, :] * u_ext[pl.ds(CARRY_ROWS - 2, tm), :]
            + cw_ref[1:2, :] * u_ext[pl.ds(CARRY_ROWS - 1, tm), :]
            + cw_ref[2:3, :] * u + cb_ref[...])
    u_ext[0:CARRY_ROWS, :] = u_ext[tm:tm + CARRY_ROWS, :]
    yc = z_cols(3) * conv
    yc_ref[...] = _group_rms(yc, cn_ref[...], ind).astype(BF16)


def _post_kernel(x1_ref, o_ref, yc_ref, p_ref, woa_ref, wob_ref, g2_ref, wg_ref, wu_ref, wd_ref,
                 gp_ref, wpg_ref, wpp_ref, out_ref):
    x2 = x1_ref[...] + _dot(o_ref[...], woa_ref[...]) + _dot(yc_ref[...], wob_ref[...])
    x3 = _swiglu_half_step(x2, g2_ref, wg_ref, wu_ref, wd_ref)
    gate = jax.nn.sigmoid(_dot(_rms(x3, gp_ref[...]).astype(BF16), wpg_ref[...]))
    out_ref[...] = x3 + gate * _dot(p_ref[...].astype(BF16), wpp_ref[...])


def _attn_kernel(qt_ref, k_ref, vt_ref, lq1_ref, lk1_ref, lq2_ref, lk2_ref, on_ref, o_ref,
                 qm, s_buf, mx_buf, p_buf, a_buf, m_sc, acc, *, lam_init):
    tq, tk = Q_TILE, KV_TILE
    i = pl.program_id(1)
    qt = qt_ref[...]
    feat = lax.broadcasted_iota(jnp.int32, qt.shape, 0)
    zero = jnp.zeros_like(qt)
    qm[:, 0:tq] = jnp.where(feat < QK_DIM, qt, zero)
    qm[:, tq:2 * tq] = jnp.where(feat >= QK_DIM, qt, zero)
    m_sc[...] = jnp.full(m_sc.shape, -jnp.inf, F32)
    acc[...] = jnp.zeros(acc.shape, F32)

    def visible_rows(lanes, diag):
        if diag is None:
            return tk
        return max(0, min(tk, (lanes.start % tq) + LANE_BLOCK - diag * tk))

    def straddles(lanes, diag):
        return diag is not None and 0 < (lanes.start % tq) + LANE_BLOCK - diag * tk <= tk

    def scores(tile, lanes, diag):
        rows = visible_rows(lanes, diag)
        if rows == 0:
            return
        start = pl.multiple_of(tile * tk, tk)
        st = _dot(k_ref[pl.ds(start, rows), :], qm[:, lanes])
        if straddles(lanes, diag):
            below, edge = st[:rows - LANE_BLOCK], st[rows - LANE_BLOCK:]
            kpos = lax.broadcasted_iota(jnp.int32, edge.shape, 0)
            qpos = lax.broadcasted_iota(jnp.int32, edge.shape, 1)
            edge = jnp.where(kpos <= qpos, edge, -jnp.inf)
            s_buf[rows - LANE_BLOCK:rows, lanes] = edge
            mx = jnp.max(edge, axis=0, keepdims=True)
            if rows > LANE_BLOCK:
                s_buf[0:rows - LANE_BLOCK, lanes] = below
                mx = jnp.maximum(mx, jnp.max(below, axis=0, keepdims=True))
            mx_buf[:, lanes] = mx
        else:
            s_buf[0:rows, lanes] = st
            mx_buf[:, lanes] = jnp.max(st, axis=0, keepdims=True)

    def softmax(lanes, diag):
        rows = visible_rows(lanes, diag)
        if rows == 0:
            return
        m_prev = m_sc[:, lanes]
        m_new = jnp.maximum(m_prev, mx_buf[:, lanes])
        alpha = jnp.exp2(m_prev - m_new)
        p_buf[0:rows, lanes] = jnp.exp2((s_buf[0:rows, lanes] - m_new).astype(BF16))
        a_buf[:, lanes] = alpha
        m_sc[:, lanes] = m_new

    def value(tile, lanes, diag):
        rows = visible_rows(lanes, diag)
        if rows == 0:
            return
        start = pl.multiple_of(tile * tk, tk)
        acc[:, lanes] = (acc[:, lanes] * a_buf[:, lanes]
                         + _dot(vt_ref[:, pl.ds(start, rows)], p_buf[0:rows, lanes]))

    def step(value_of=None, softmax_of=None, scores_of=None):
        for blk in range(2 * tq // LANE_BLOCK):
            lanes = slice(blk * LANE_BLOCK, (blk + 1) * LANE_BLOCK)
            if value_of is not None:
                value(value_of[0], lanes, value_of[1])
            if softmax_of is not None:
                softmax(lanes, softmax_of[1])
            if scores_of is not None:
                scores(scores_of[0], lanes, scores_of[1])

    n_diag = tq // tk
    first_diag = i * n_diag
    n_items = n_diag + first_diag

    def item(j):
        if isinstance(j, int) and j < n_diag:
            return (first_diag + j, j)
        return (j - n_diag, None)

    def run_step(s, last_item):
        def stage(j):
            return item(j) if 0 <= j <= last_item else None
        step(value_of=stage(s - 2), softmax_of=stage(s - 1), scores_of=stage(s))

    for s in range(n_diag):
        run_step(s, n_diag - 1)

    @pl.when(i == 0)
    def _():
        run_step(n_diag, n_diag - 1)
        run_step(n_diag + 1, n_diag - 1)

    @pl.when(i > 0)
    def _():
        run_step(n_diag, n_diag + 1)
        run_step(n_diag + 1, n_diag + 1)

        def body(s, carry):
            step(value_of=item(s - 2), softmax_of=item(s - 1), scores_of=item(s))
            return carry

        lax.fori_loop(n_diag + 2, n_items, body, 0)
        step(value_of=item(n_items - 2), softmax_of=item(n_items - 1))
        step(value_of=item(n_items - 1))

    lam = (jnp.exp(jnp.sum(lq1_ref[...] * lk1_ref[...], axis=-1, keepdims=True))
           - jnp.exp(jnp.sum(lq2_ref[...] * lk2_ref[...], axis=-1, keepdims=True)) + lam_init)
    num, den = acc[0:HEAD_DIM, :], acc[HEAD_DIM:HEAD_DIM + 1, :]
    ot = num[:, 0:tq] / den[:, 0:tq] - lam * (num[:, tq:2 * tq] / den[:, tq:2 * tq])
    ot = ot * lax.rsqrt(jnp.mean(ot * ot, axis=0, keepdims=True) + EPS) * on_ref[...]
    o_ref[...] = (ot * (1.0 - lam_init)).T.astype(BF16)


def _resident(a):
    if isinstance(a, tuple):
        _, block_shape, block_index = a
        return pl.BlockSpec(block_shape, lambda *_: block_index, pipeline_mode=pl.Buffered(1))
    return pl.BlockSpec(a.shape, lambda *_: (0,) * a.ndim, pipeline_mode=pl.Buffered(1))


def _operand(a):
    return a[0] if isinstance(a, tuple) else a


def _layer(stacked, layer):
    return (stacked, (None,) + stacked.shape[1:], (layer, 0, 0))


def _rows(width, tm=ROW_TILE):
    return pl.BlockSpec((tm, width), lambda i: (i, 0))


def _pre_call(x, g1, wg, wu, wd, gm, win, qn, kn, cw, cb, cn, ind):
    s = x.shape[0]
    row_major = pl.BlockSpec((N_HEADS, ROW_TILE, HEAD_DIM), lambda i: (0, i, 0))
    feat_major = pl.BlockSpec((N_HEADS, HEAD_DIM, ROW_TILE), lambda i: (0, 0, i))
    v_major = pl.BlockSpec((N_HEADS, V_ROWS, ROW_TILE), lambda i: (0, 0, i))
    consts = (g1, wg, wu, wd, gm, win, qn, kn, cw, cb, cn, ind)
    return pl.pallas_call(
        _pre_kernel,
        grid=(s // ROW_TILE,),
        in_specs=[_rows(D_MODEL)] + [_resident(c) for c in consts],
        out_specs=[_rows(D_MODEL), feat_major, row_major, v_major, _rows(CONV_WIDTH)],
        out_shape=[jax.ShapeDtypeStruct((s, D_MODEL), F32),
                   jax.ShapeDtypeStruct((N_HEADS, HEAD_DIM, s), BF16),
                   jax.ShapeDtypeStruct((N_HEADS, s, HEAD_DIM), BF16),
                   jax.ShapeDtypeStruct((N_HEADS, V_ROWS, s), BF16),
                   jax.ShapeDtypeStruct((s, CONV_WIDTH), BF16)],
        scratch_shapes=[pltpu.VMEM((ROW_TILE + CARRY_ROWS, CONV_WIDTH), F32)],
        compiler_params=pltpu.CompilerParams(
            dimension_semantics=("arbitrary",), vmem_limit_bytes=VMEM_LIMIT_BYTES),
        name="pre_mix",
    )(x, *map(_operand, consts))


def _post_call(x1, o, yc, p, layer, woa, wob, g2, wg, wu, wd, gp, wpg, wpp):
    s = x1.shape[0]
    consts = (woa, wob, g2, wg, wu, wd, gp, wpg, wpp)
    p_rows = pl.BlockSpec((None, None, ROW_TILE, D_PLE), lambda i: (layer, 0, i, 0))
    return pl.pallas_call(
        _post_kernel,
        grid=(s // ROW_TILE,),
        in_specs=[_rows(D_MODEL), _rows(ATTN_WIDTH), _rows(CONV_WIDTH), p_rows]
        + [_resident(c) for c in consts],
        out_specs=_rows(D_MODEL),
        out_shape=jax.ShapeDtypeStruct((s, D_MODEL), F32),
        compiler_params=pltpu.CompilerParams(
            dimension_semantics=("parallel",), vmem_limit_bytes=VMEM_LIMIT_BYTES),
        name="post_mix",
    )(x1, o, yc, p, *map(_operand, consts))


def _attn_call(qt, k, vt, lq1, lk1, lq2, lk2, on, lam_init):
    s = k.shape[1]
    qt_spec = pl.BlockSpec((None, HEAD_DIM, Q_TILE), lambda h, i: (h, 0, i))
    k_spec = pl.BlockSpec((None, s, HEAD_DIM), lambda h, i: (h, 0, 0))
    vt_spec = pl.BlockSpec((None, V_ROWS, s), lambda h, i: (h, 0, 0))
    vec = pl.BlockSpec((1, QK_DIM), lambda h, i: (0, 0))
    return pl.pallas_call(
        functools.partial(_attn_kernel, lam_init=lam_init),
        grid=(N_HEADS, s // Q_TILE),
        in_specs=[qt_spec, k_spec, vt_spec, vec, vec, vec, vec,
                  pl.BlockSpec((HEAD_DIM, 1), lambda h, i: (0, 0))],
        out_specs=pl.BlockSpec((Q_TILE, HEAD_DIM), lambda h, i: (i, h)),
        out_shape=jax.ShapeDtypeStruct((s, ATTN_WIDTH), BF16),
        scratch_shapes=[pltpu.VMEM((HEAD_DIM, 2 * Q_TILE), BF16),
                        pltpu.VMEM((KV_TILE, 2 * Q_TILE), F32),
                        pltpu.VMEM((1, 2 * Q_TILE), F32),
                        pltpu.VMEM((KV_TILE, 2 * Q_TILE), BF16),
                        pltpu.VMEM((1, 2 * Q_TILE), F32),
                        pltpu.VMEM((1, 2 * Q_TILE), F32),
                        pltpu.VMEM((V_ROWS, 2 * Q_TILE), F32)],
        compiler_params=pltpu.CompilerParams(
            dimension_semantics=("parallel", "parallel"), vmem_limit_bytes=VMEM_LIMIT_BYTES),
        name="diff_attn",
    )(qt, k, vt, lq1, lk1, lq2, lk2, on)


def kernel(x, p, ffn1_norm, ffn1_w_gate, ffn1_w_up, ffn1_w_down, mix_norm, w_in, q_norm, k_norm,
           lambda_q1, lambda_k1, lambda_q2, lambda_k2, attn_out_norm, conv_w, conv_b, conv_out_norm,
           w_out, ffn2_norm, ffn2_w_gate, ffn2_w_up, ffn2_w_down, ple_norm, ple_w_gate, ple_w_proj):
    b, s, _ = x.shape
    depth = p.shape[0]
    assert b == 1 and s % Q_TILE == 0 and s % ROW_TILE == 0 and Q_TILE % KV_TILE == 0
    group = jnp.arange(MXU_TILE, dtype=jnp.int32) // GROUP_DIM
    ind = (group[:, None] == group[None, :]).astype(BF16)

    def row(a):
        return a.reshape(1, -1).astype(F32)

    def tiled(a):
        return jnp.tile(a.astype(F32), ATTN_WIDTH // a.shape[0]).reshape(1, ATTN_WIDTH)

    wg1, wu1, wd1, win, wo, wg2, wu2, wd2, wpg, wpp = (
        w.astype(BF16) for w in (ffn1_w_gate, ffn1_w_up, ffn1_w_down, w_in, w_out,
                                 ffn2_w_gate, ffn2_w_up, ffn2_w_down, ple_w_gate, ple_w_proj))
    xs = x.reshape(s, D_MODEL)
    for i in range(depth):
        lam_init = 0.8 - 0.6 * math.exp(-0.3 * i)
        x1, qt, k, vt, yc = _pre_call(
            xs, row(ffn1_norm[i]), _layer(wg1, i), _layer(wu1, i), _layer(wd1, i), row(mix_norm[i]),
            _layer(win, i), tiled(q_norm[i]), tiled(k_norm[i]), conv_w[i].astype(F32),
            row(conv_b[i]), row(conv_out_norm[i]), ind)
        o = _attn_call(qt, k, vt, row(lambda_q1[i]), row(lambda_k1[i]), row(lambda_q2[i]),
                       row(lambda_k2[i]), attn_out_norm[i].astype(F32).reshape(HEAD_DIM, 1), lam_init)
        wo_half = (None, ATTN_WIDTH, D_MODEL)
        xs = _post_call(
            x1, o, yc, p, i, (wo, wo_half, (i, 0, 0)), (wo, wo_half, (i, 1, 0)),
            row(ffn2_norm[i]), _layer(wg2, i), _layer(wu2, i), _layer(wd2, i), row(ple_norm[i]),
            _layer(wpg, i), _layer(wpp, i))
    return xs.reshape(b, s, D_MODEL)
```

```python
import functools
import math

import jax
import jax.numpy as jnp
from jax import lax
from jax.experimental import pallas as pl
from jax.experimental.pallas import tpu as pltpu

D_MODEL = 1024
D_FF = 2816
D_PLE = 256
ATTN_WIDTH = 512
CONV_WIDTH = 512
N_HEADS = 4
HEAD_DIM = 128
BF16_TILE_ROWS = 16
V_ROWS = HEAD_DIM + BF16_TILE_ROWS
QK_DIM = 64
GROUP_DIM = 64
CONV_K = 3
EPS = 1e-6

ROW_TILE = 512
Q_TILE = 2048
KV_TILE = 1024
MXU_TILE = 256
LANE_BLOCK = MXU_TILE
CARRY_ROWS = 8
V7X_VMEM_BYTES = 64 * 1024 * 1024
VMEM_LIMIT_BYTES = V7X_VMEM_BYTES * 7 // 8

BF16 = jnp.bfloat16
F32 = jnp.float32


def _dot(a, b):
    return jnp.dot(a, b, preferred_element_type=F32)


def _rms(x, g):
    return x * lax.rsqrt(jnp.mean(x * x, axis=-1, keepdims=True) + EPS) * g


def _group_rms(x, g, ind):
    sq = (x * x).astype(BF16)
    sums = [_dot(sq[:, c:c + MXU_TILE], ind) for c in range(0, x.shape[1], MXU_TILE)]
    ms = jnp.concatenate(sums, axis=1) * (1.0 / GROUP_DIM)
    return x * lax.rsqrt(ms + EPS) * g


def _swiglu_half_step(x, g_ref, wg_ref, wu_ref, wd_ref):
    n = _rms(x, g_ref[...]).astype(BF16)
    gate = _dot(n, wg_ref[...])
    up = _dot(n, wu_ref[...])
    h = (gate * jax.nn.sigmoid(gate) * up).astype(BF16)
    return x + 0.5 * _dot(h, wd_ref[...])


def _pre_kernel(x_ref, g1_ref, wg_ref, wu_ref, wd_ref, gm_ref, win_ref, qn_ref, kn_ref,
                cw_ref, cb_ref, cn_ref, ind_ref,
                x1_ref, q_ref, k_ref, v_ref, yc_ref, u_ext):
    tm = x_ref.shape[0]
    x1 = _swiglu_half_step(x_ref[...], g1_ref, wg_ref, wu_ref, wd_ref)
    x1_ref[...] = x1
    h = _rms(x1, gm_ref[...]).astype(BF16)
    ind = ind_ref[...]

    def z_cols(section):
        return _dot(h, win_ref[:, section * ATTN_WIDTH:(section + 1) * ATTN_WIDTH])

    q = _group_rms(z_cols(0), qn_ref[...], ind) * (math.log2(math.e) / math.sqrt(QK_DIM))
    k = _group_rms(z_cols(1), kn_ref[...], ind)
    v = z_cols(2)
    for hd in range(N_HEADS):
        cols = slice(hd * HEAD_DIM, (hd + 1) * HEAD_DIM)
        q_ref[hd] = q[:, cols].T.astype(BF16)
        k_ref[hd] = k[:, cols].astype(BF16)
        v_ref[hd, 0:HEAD_DIM, :] = v[:, cols].T.astype(BF16)
        v_ref[hd, HEAD_DIM:V_ROWS, :] = jnp.ones((V_ROWS - HEAD_DIM, tm), BF16)

    @pl.when(pl.program_id(0) == 0)
    def _():
        u_ext[0:CARRY_ROWS, :] = jnp.zeros((CARRY_ROWS, CONV_WIDTH), F32)

    u = z_cols(4) * z_cols(5)
    u_ext[CARRY_ROWS:CARRY_ROWS + tm, :] = u
    conv = (cw_ref[0:1, :] * u_ext[pl.ds(CARRY_ROWS - 2, tm), :]
            + cw_ref[1:2, :] * u_ext[pl.ds(CARRY_ROWS - 1, tm), :]
            + cw_ref[2:3, :] * u + cb_ref[...])
    u_ext[0:CARRY_ROWS, :] = u_ext[tm:tm + CARRY_ROWS, :]
    yc = z_cols(3) * conv
    yc_ref[...] = _group_rms(yc, cn_ref[...], ind).astype(BF16)


def _post_kernel(x1_ref, o_ref, yc_ref, p_ref, woa_ref, wob_ref, g2_ref, wg_ref, wu_ref, wd_ref,
                 gp_ref, wpg_ref, wpp_ref, out_ref):
    x2 = x1_ref[...] + _dot(o_ref[...], woa_ref[...]) + _dot(yc_ref[...], wob_ref[...])
    x3 = _swiglu_half_step(x2, g2_ref, wg_ref, wu_ref, wd_ref)
    gate = jax.nn.sigmoid(_dot(_rms(x3, gp_ref[...]).astype(BF16), wpg_ref[...]))
    out_ref[...] = x3 + gate * _dot(p_ref[...].astype(BF16), wpp_ref[...])


def _attn_kernel(qt_ref, k_ref, vt_ref, lq1_ref, lk1_ref, lq2_ref, lk2_ref, on_ref, o_ref,
                 qm, s_buf, mx_buf, p_buf, a_buf, m_sc, acc, *, lam_init):
    tq, tk = Q_TILE, KV_TILE
    i = pl.program_id(1)
    qt = qt_ref[...]
    feat = lax.broadcasted_iota(jnp.int32, qt.shape, 0)
    zero = jnp.zeros_like(qt)
    qm[:, 0:tq] = jnp.where(feat < QK_DIM, qt, zero)
    qm[:, tq:2 * tq] = jnp.where(feat >= QK_DIM, qt, zero)
    m_sc[...] = jnp.full(m_sc.shape, -jnp.inf, F32)
    acc[...] = jnp.zeros(acc.shape, F32)

    def visible_rows(lanes, diag):
        if diag is None:
            return tk
        return max(0, min(tk, (lanes.start % tq) + LANE_BLOCK - diag * tk))

    def straddles(lanes, diag):
        return diag is not None and 0 < (lanes.start % tq) + LANE_BLOCK - diag * tk <= tk

    def scores(tile, lanes, diag):
        rows = visible_rows(lanes, diag)
        if rows == 0:
            return
        start = pl.multiple_of(tile * tk, tk)
        st = _dot(k_ref[pl.ds(start, rows), :], qm[:, lanes])
        if straddles(lanes, diag):
            below, edge = st[:rows - LANE_BLOCK], st[rows - LANE_BLOCK:]
            kpos = lax.broadcasted_iota(jnp.int32, edge.shape, 0)
            qpos = lax.broadcasted_iota(jnp.int32, edge.shape, 1)
            edge = jnp.where(kpos <= qpos, edge, -jnp.inf)
            s_buf[rows - LANE_BLOCK:rows, lanes] = edge
            mx = jnp.max(edge, axis=0, keepdims=True)
            if rows > LANE_BLOCK:
                s_buf[0:rows - LANE_BLOCK, lanes] = below
                mx = jnp.maximum(mx, jnp.max(below, axis=0, keepdims=True))
            mx_buf[:, lanes] = mx
        else:
            s_buf[0:rows, lanes] = st
            mx_buf[:, lanes] = jnp.max(st, axis=0, keepdims=True)

    def softmax(lanes, diag):
        rows = visible_rows(lanes, diag)
        if rows == 0:
            return
        m_prev = m_sc[:, lanes]
        m_new = jnp.maximum(m_prev, mx_buf[:, lanes])
        alpha = jnp.exp2(m_prev - m_new)
        p_buf[0:rows, lanes] = jnp.exp2((s_buf[0:rows, lanes] - m_new).astype(BF16))
        a_buf[:, lanes] = alpha
        m_sc[:, lanes] = m_new

    def value(tile, lanes, diag):
        rows = visible_rows(lanes, diag)
        if rows == 0:
            return
        start = pl.multiple_of(tile * tk, tk)
        acc[:, lanes] = (acc[:, lanes] * a_buf[:, lanes]
                         + _dot(vt_ref[:, pl.ds(start, rows)], p_buf[0:rows, lanes]))

    def step(value_of=None, softmax_of=None, scores_of=None):
        for blk in range(2 * tq // LANE_BLOCK):
            lanes = slice(blk * LANE_BLOCK, (blk + 1) * LANE_BLOCK)
            if value_of is not None:
                value(value_of[0], lanes, value_of[1])
            if softmax_of is not None:
                softmax(lanes, softmax_of[1])
            if scores_of is not None:
                scores(scores_of[0], lanes, scores_of[1])

    n_diag = tq // tk
    first_diag = i * n_diag
    n_items = n_diag + first_diag

    def item(j):
        if isinstance(j, int) and j < n_diag:
            return (first_diag + j, j)
        return (j - n_diag, None)

    def run_step(s, last_item):
        def stage(j):
            return item(j) if 0 <= j <= last_item else None
        step(value_of=stage(s - 2), softmax_of=stage(s - 1), scores_of=stage(s))

    for s in range(n_diag):
        run_step(s, n_diag - 1)

    @pl.when(i == 0)
    def _():
        run_step(n_diag, n_diag - 1)
        run_step(n_diag + 1, n_diag - 1)

    @pl.when(i > 0)
    def _():
        run_step(n_diag, n_diag + 1)
        run_step(n_diag + 1, n_diag + 1)

        def body(s, carry):
            step(value_of=item(s - 2), softmax_of=item(s - 1), scores_of=item(s))
            return carry

        lax.fori_loop(n_diag + 2, n_items, body, 0)
        step(value_of=item(n_items - 2), softmax_of=item(n_items - 1))
        step(value_of=item(n_items - 1))

    lam = (jnp.exp(jnp.sum(lq1_ref[...] * lk1_ref[...], axis=-1, keepdims=True))
           - jnp.exp(jnp.sum(lq2_ref[...] * lk2_ref[...], axis=-1, keepdims=True)) + lam_init)
    num, den = acc[0:HEAD_DIM, :], acc[HEAD_DIM:HEAD_DIM + 1, :]
    ot = num[:, 0:tq] / den[:, 0:tq] - lam * (num[:, tq:2 * tq] / den[:, tq:2 * tq])
    ot = ot * lax.rsqrt(jnp.mean(ot * ot, axis=0, keepdims=True) + EPS) * on_ref[...]
    o_ref[...] = (ot * (1.0 - lam_init)).T.astype(BF16)


def _resident(a):
    if isinstance(a, tuple):
        _, block_shape, block_index = a
        return pl.BlockSpec(block_shape, lambda *_: block_index, pipeline_mode=pl.Buffered(1))
    return pl.BlockSpec(a.shape, lambda *_: (0,) * a.ndim, pipeline_mode=pl.Buffered(1))


def _operand(a):
    return a[0] if isinstance(a, tuple) else a


def _layer(stacked, layer):
    return (stacked, (None,) + stacked.shape[1:], (layer, 0, 0))


def _rows(width, tm=ROW_TILE):
    return pl.BlockSpec((tm, width), lambda i: (i, 0))


def _pre_call(x, g1, wg, wu, wd, gm, win, qn, kn, cw, cb, cn, ind):
    s = x.shape[0]
    row_major = pl.BlockSpec((N_HEADS, ROW_TILE, HEAD_DIM), lambda i: (0, i, 0))
    feat_major = pl.BlockSpec((N_HEADS, HEAD_DIM, ROW_TILE), lambda i: (0, 0, i))
    v_major = pl.BlockSpec((N_HEADS, V_ROWS, ROW_TILE), lambda i: (0, 0, i))
    consts = (g1, wg, wu, wd, gm, win, qn, kn, cw, cb, cn, ind)
    return pl.pallas_call(
        _pre_kernel,
        grid=(s // ROW_TILE,),
        in_specs=[_rows(D_MODEL)] + [_resident(c) for c in consts],
        out_specs=[_rows(D_MODEL), feat_major, row_major, v_major, _rows(CONV_WIDTH)],
        out_shape=[jax.ShapeDtypeStruct((s, D_MODEL), F32),
                   jax.ShapeDtypeStruct((N_HEADS, HEAD_DIM, s), BF16),
                   jax.ShapeDtypeStruct((N_HEADS, s, HEAD_DIM), BF16),
                   jax.ShapeDtypeStruct((N_HEADS, V_ROWS, s), BF16),
                   jax.ShapeDtypeStruct((s, CONV_WIDTH), BF16)],
        scratch_shapes=[pltpu.VMEM((ROW_TILE + CARRY_ROWS, CONV_WIDTH), F32)],
        compiler_params=pltpu.CompilerParams(
            dimension_semantics=("arbitrary",), vmem_limit_bytes=VMEM_LIMIT_BYTES),
        name="pre_mix",
    )(x, *map(_operand, consts))


def _post_call(x1, o, yc, p, layer, woa, wob, g2, wg, wu, wd, gp, wpg, wpp):
    s = x1.shape[0]
    consts = (woa, wob, g2, wg, wu, wd, gp, wpg, wpp)
    p_rows = pl.BlockSpec((None, None, ROW_TILE, D_PLE), lambda i: (layer, 0, i, 0))
    return pl.pallas_call(
        _post_kernel,
        grid=(s // ROW_TILE,),
        in_specs=[_rows(D_MODEL), _rows(ATTN_WIDTH), _rows(CONV_WIDTH), p_rows]
        + [_resident(c) for c in consts],
        out_specs=_rows(D_MODEL),
        out_shape=jax.ShapeDtypeStruct((s, D_MODEL), F32),
        compiler_params=pltpu.CompilerParams(
            dimension_semantics=("parallel",), vmem_limit_bytes=VMEM_LIMIT_BYTES),
        name="post_mix",
    )(x1, o, yc, p, *map(_operand, consts))


def _attn_call(qt, k, vt, lq1, lk1, lq2, lk2, on, lam_init):
    s = k.shape[1]
    qt_spec = pl.BlockSpec((None, HEAD_DIM, Q_TILE), lambda h, i: (h, 0, i))
    k_spec = pl.BlockSpec((None, s, HEAD_DIM), lambda h, i: (h, 0, 0))
    vt_spec = pl.BlockSpec((None, V_ROWS, s), lambda h, i: (h, 0, 0))
    vec = pl.BlockSpec((1, QK_DIM), lambda h, i: (0, 0))
    return pl.pallas_call(
        functools.partial(_attn_kernel, lam_init=lam_init),
        grid=(N_HEADS, s // Q_TILE),
        in_specs=[qt_spec, k_spec, vt_spec, vec, vec, vec, vec,
                  pl.BlockSpec((HEAD_DIM, 1), lambda h, i: (0, 0))],
        out_specs=pl.BlockSpec((Q_TILE, HEAD_DIM), lambda h, i: (i, h)),
        out_shape=jax.ShapeDtypeStruct((s, ATTN_WIDTH), BF16),
        scratch_shapes=[pltpu.VMEM((HEAD_DIM, 2 * Q_TILE), BF16),
                        pltpu.VMEM((KV_TILE, 2 * Q_TILE), F32),
                        pltpu.VMEM((1, 2 * Q_TILE), F32),
                        pltpu.VMEM((KV_TILE, 2 * Q_TILE), BF16),
                        pltpu.VMEM((1, 2 * Q_TILE), F32),
                        pltpu.VMEM((1, 2 * Q_TILE), F32),
                        pltpu.VMEM((V_ROWS, 2 * Q_TILE), F32)],
        compiler_params=pltpu.CompilerParams(
            dimension_semantics=("parallel", "parallel"), vmem_limit_bytes=VMEM_LIMIT_BYTES),
        name="diff_attn",
    )(qt, k, vt, lq1, lk1, lq2, lk2, on)


def kernel(x, p, ffn1_norm, ffn1_w_gate, ffn1_w_up, ffn1_w_down, mix_norm, w_in, q_norm, k_norm,
           lambda_q1, lambda_k1, lambda_q2, lambda_k2, attn_out_norm, conv_w, conv_b, conv_out_norm,
           w_out, ffn2_norm, ffn2_w_gate, ffn2_w_up, ffn2_w_down, ple_norm, ple_w_gate, ple_w_proj):
    b, s, _ = x.shape
    depth = p.shape[0]
    assert b == 1 and s % Q_TILE == 0 and s % ROW_TILE == 0 and Q_TILE % KV_TILE == 0
    group = jnp.arange(MXU_TILE, dtype=jnp.int32) // GROUP_DIM
    ind = (group[:, None] == group[None, :]).astype(BF16)

    def row(a):
        return a.reshape(1, -1).astype(F32)

    def tiled(a):
        return jnp.tile(a.astype(F32), ATTN_WIDTH // a.shape[0]).reshape(1, ATTN_WIDTH)

    wg1, wu1, wd1, win, wo, wg2, wu2, wd2, wpg, wpp = (
        w.astype(BF16) for w in (ffn1_w_gate, ffn1_w_up, ffn1_w_down, w_in, w_out,
                                 ffn2_w_gate, ffn2_w_up, ffn2_w_down, ple_w_gate, ple_w_proj))
    xs = x.reshape(s, D_MODEL)
    for i in range(depth):
        lam_init = 0.8 - 0.6 * math.exp(-0.3 * i)
        x1, qt, k, vt, yc = _pre_call(
            xs, row(ffn1_norm[i]), _layer(wg1, i), _layer(wu1, i), _layer(wd1, i), row(mix_norm[i]),
            _layer(win, i), tiled(q_norm[i]), tiled(k_norm[i]), conv_w[i].astype(F32),
            row(conv_b[i]), row(conv_out_norm[i]), ind)
        o = _attn_call(qt, k, vt, row(lambda_q1[i]), row(lambda_k1[i]), row(lambda_q2[i]),
                       row(lambda_k2[i]), attn_out_norm[i].astype(F32).reshape(HEAD_DIM, 1), lam_init)
        wo_half = (None, ATTN_WIDTH, D_MODEL)
        xs = _post_call(
            x1, o, yc, p, i, (wo, wo_half, (i, 0, 0)), (wo, wo_half, (i, 1, 0)),
            row(ffn2_norm[i]), _layer(wg2, i), _layer(wu2, i), _layer(wd2, i), row(ple_norm[i]),
            _layer(wpg, i), _layer(wpp, i))
    return xs.reshape(b, s, D_MODEL)
```
